```python
import math
import jax
import jax.numpy as jnp
from jax import lax
import numpy as np

D_MODEL = 1024
BATCH = 16
SEQ = 2048
DEPTH = 4

GRID_W = 64
CTX_LEN = 256
N_EVEN = (DEPTH + 1) // 2
N_ODD = DEPTH // 2
N_MOD = 6

DN_ALPHA = (2.0 * DEPTH) ** 0.25
DN_BETA = (8.0 * DEPTH) ** -0.25
LN_EPS = 1e-5

HY_W = D_MODEL // 2
HY_ORDER = 2
HY_SHORT = 3
HY_EMB = 33
HY_BANDS = (HY_EMB - 1) // 2
HY_HID = 64
HY_N_FILT = HY_ORDER * 2 * HY_W
HY_DECAY_TARGET = 1e-2
HY_FAST_PCT = 0.3
HY_SLOW_PCT = 1.5

ATT_W = D_MODEL // 2
DA_HEADS = 4
DA_HD = ATT_W // (2 * DA_HEADS)
DA_VD = 2 * DA_HD
ROPE_BASE = 10000.0
ROPE_NF = DA_HD // 4
Q_BLOCK = 128
AB_IN = 3 * HY_W + 3 * ATT_W
AB_OUT = HY_W + ATT_W

D_RNN = D_MODEL
RG_BLOCKS = 8
RG_BW = D_RNN // RG_BLOCKS
RG_CONV = 4
RG_C = 8.0

N_EXPERTS = 64
N_GROUPS = 8
EPG = N_EXPERTS // N_GROUPS
TOPK_GROUPS = 4
TOP_K = 8
D_EXPERT = D_MODEL // 4
D_SHARED = D_MODEL // 4
ROUTED_SCALE = 2.5

F32 = jnp.float32

kernel_name = 'hybrid_hyena_diffattn_rglru_moe_dit'


def layer_norm(x, g, b):
    xf = x.astype(F32)
    mu = jnp.mean(xf, -1, keepdims=True)
    var = jnp.mean(jnp.square(xf - mu), -1, keepdims=True)
    return ((xf - mu) * lax.rsqrt(var + LN_EPS) * g + b).astype(x.dtype)


def rms_norm(x, g):
    xf = x.astype(F32)
    return (xf * lax.rsqrt(jnp.mean(xf * xf, -1, keepdims=True) + LN_EPS) * g).astype(x.dtype)


def depthwise_conv_centred(u, w, b):
    k = w.shape[0]
    left = k // 2
    y = lax.conv_general_dilated(u, w[:, None, :].astype(u.dtype), window_strides=(1,),
                                 padding=[(left, k - 1 - left)],
                                 dimension_numbers=('NWC', 'WIO', 'NWC'),
                                 feature_group_count=u.shape[-1])
    return y + b


def hyena_filters(n, w1, b1, w2, b2, w3, b3, w4, freq, decay):
    t = jnp.linspace(0.0, 1.0, n, dtype=F32)[:, None]
    f = jnp.linspace(1e-4, HY_BANDS - 1, HY_BANDS, dtype=F32)
    ang = (2.0 * math.pi / n) * jnp.arange(n, dtype=F32)[:, None] * f[None, :]
    z = jnp.concatenate([t, jnp.cos(ang), -jnp.sin(ang)], axis=-1)
    fr = freq.astype(F32)
    hdn = jnp.sin(fr * (z @ w1.astype(F32) + b1.astype(F32)))
    hdn = jnp.sin(fr * (hdn @ w2.astype(F32) + b2.astype(F32)))
    hdn = jnp.sin(fr * (hdn @ w3.astype(F32) + b3.astype(F32)))
    h = (hdn @ w4.astype(F32)) * jnp.exp(-t * jnp.abs(decay.astype(F32)))
    return h.reshape(n, HY_ORDER, 2, HY_W)


def long_conv_bidir(u, h_fwd, h_bwd, skip):
    n = u.shape[1]
    kern = jnp.concatenate([h_fwd[:1] + h_bwd[:1], h_fwd[1:], jnp.zeros_like(h_fwd[:1]), h_bwd[:0:-1]], axis=0)
    uf = u.astype(F32)
    spec = jnp.fft.rfft(uf, n=2 * n, axis=1) * jnp.fft.rfft(kern, axis=0)[None]
    y = jnp.fft.irfft(spec, n=2 * n, axis=1)[:, :n]
    return y + uf * skip.astype(F32)


def hyena_mixer(u, short_w, short_b, filt, skip):
    uc = depthwise_conv_centred(u, short_w, short_b).astype(F32)
    v, x1, x2 = jnp.split(uc, 3, axis=-1)
    z = x1 * long_conv_bidir(v, filt[:, 0, 0], filt[:, 0, 1], skip[0])
    y = x2 * long_conv_bidir(z, filt[:, 1, 0], filt[:, 1, 1], skip[1])
    return y.astype(u.dtype)


def axial_rope_tables(n):
    rows = n // GRID_W
    r = jnp.repeat(jnp.arange(rows, dtype=F32), GRID_W)
    col = jnp.tile(jnp.arange(GRID_W, dtype=F32), rows)
    inv = ROPE_BASE ** (-jnp.arange(ROPE_NF, dtype=F32) / ROPE_NF)
    ang = jnp.stack([r[:, None] * inv, col[:, None] * inv], axis=1)
    return jnp.cos(ang), jnp.sin(ang)


def apply_axial_rope(x, cos, sin):
    xs = x.reshape(x.shape[:-1] + (2, 2, ROPE_NF)).astype(F32)
    xa, xb = xs[..., 0, :], xs[..., 1, :]
    cc = cos[None, :, None, None]
    ss = sin[None, :, None, None]
    out = jnp.stack([xa * cc - xb * ss, xb * cc + xa * ss], axis=-2)
    return out.reshape(x.shape).astype(x.dtype)


def diff_attend(q, k, v, lam):
    s = jnp.einsum('bqhsd,bkhsd->bhsqk', q, k).astype(F32) * (DA_HD ** -0.5)
    p = jax.nn.softmax(s, axis=-1)
    w = p[:, :, 0] - lam * p[:, :, 1]
    return jnp.einsum('bhqk,bkhe->bqhe', w.astype(v.dtype), v)


def ab_mixer(h_ctx, h_lat, w_in, w_out, short_w, short_b, filt_p, skip,
             lq1, lk1, lq2, lk2, subln_g, lam_init, need_ctx):
    bsz, n_lat, _ = h_lat.shape
    n_ctx = h_ctx.shape[1]
    hy = 3 * HY_W
    p_ctx = h_ctx @ w_in
    p_lat = h_lat @ w_in

    y_hy_lat = hyena_mixer(p_lat[..., :hy], short_w, short_b, hyena_filters(n_lat, *filt_p), skip)

    def split_qkv(p, n):
        q, k, v = jnp.split(p[..., hy:], 3, axis=-1)
        return (q.reshape(bsz, n, DA_HEADS, 2, DA_HD), k.reshape(bsz, n, DA_HEADS, 2, DA_HD),
                v.reshape(bsz, n, DA_HEADS, DA_VD))

    q_c, k_c, v_c = split_qkv(p_ctx, n_ctx)
    q_l, k_l, v_l = split_qkv(p_lat, n_lat)
    cos, sin = axial_rope_tables(n_lat)
    q_l = apply_axial_rope(q_l, cos, sin)
    k_l = apply_axial_rope(k_l, cos, sin)
    lam = (jnp.exp(jnp.sum(lq1.astype(F32) * lk1.astype(F32)))
           - jnp.exp(jnp.sum(lq2.astype(F32) * lk2.astype(F32))) + lam_init)

    k_all = jnp.concatenate([k_c, k_l], axis=1)
    v_all = jnp.concatenate([v_c, v_l], axis=1)
    nblk = n_lat // Q_BLOCK
    qb = jnp.moveaxis(q_l.reshape(bsz, nblk, Q_BLOCK, DA_HEADS, 2, DA_HD), 1, 0)
    ob = lax.map(lambda qq: diff_attend(qq, k_all, v_all, lam), qb)
    o_lat = jnp.moveaxis(ob, 0, 1).reshape(bsz, n_lat, DA_HEADS, DA_VD)

    def head_out(o):
        return (rms_norm(o, subln_g) * (1.0 - lam_init)).reshape(bsz, o.shape[1], ATT_W)

    m_lat = jnp.concatenate([y_hy_lat, head_out(o_lat).astype(y_hy_lat.dtype)], axis=-1) @ w_out
    if not need_ctx:
        return None, m_lat
    y_hy_ctx = hyena_mixer(p_ctx[..., :hy], short_w, short_b, hyena_filters(n_ctx, *filt_p), skip)
    o_ctx = diff_attend(q_c, k_c, v_c, lam)
    m_ctx = jnp.concatenate([y_hy_ctx, head_out(o_ctx).astype(y_hy_ctx.dtype)], axis=-1) @ w_out
    return m_ctx, m_lat


def linear_scan(a, b, h0, reverse):
    def combine(l, r):
        return l[0] * r[0], r[0] * l[1] + r[1]
    a_cum, b_cum = lax.associative_scan(combine, (a, b), axis=1, reverse=reverse)
    return a_cum * h0[:, None, :] + b_cum


def rglru_coeffs(u, wa, ba, wx, bx, lam):
    bsz, n, _ = u.shape
    ub = u.reshape(bsz, n, RG_BLOCKS, RG_BW)
    r = jax.nn.sigmoid(jnp.einsum('bnkc,kcd->bnkd', ub, wa.astype(F32)).reshape(bsz, n, D_RNN) + ba.astype(F32))
    i = jax.nn.sigmoid(jnp.einsum('bnkc,kcd->bnkd', ub, wx.astype(F32)).reshape(bsz, n, D_RNN) + bx.astype(F32))
    log_a = RG_C * r * jax.nn.log_sigmoid(lam.astype(F32))
    a = jnp.exp(log_a)
    b = jnp.sqrt(-jnp.expm1(2.0 * log_a)) * (i * u)
    return a, b


def rg_mixer(h_ctx, h_lat, w_in, w_out, conv_w, conv_b, wa, ba, wx, bx, lam, need_ctx):
    def branches(h):
        gate, rec = jnp.split(h @ w_in, 2, axis=-1)
        return jax.nn.gelu(gate), depthwise_conv_centred(rec, conv_w, conv_b).astype(F32)

    g_c, u_c = branches(h_ctx)
    g_l, u_l = branches(h_lat)

    def direction(d, reverse):
        a, b = rglru_coeffs(u_c, wa[d], ba[d], wx[d], bx[d], lam[d])
        hc = linear_scan(a, b, jnp.zeros_like(u_c[:, 0]), reverse)
        h_end = hc[:, 0] if reverse else hc[:, -1]
        a, b = rglru_coeffs(u_l, wa[d], ba[d], wx[d], bx[d], lam[d])
        return hc, linear_scan(a, b, h_end, reverse)

    hc_f, hl_f = direction(0, False)
    hc_b, hl_b = direction(1, True)
    m_lat = (g_l * (hl_f + hl_b).astype(g_l.dtype)) @ w_out
    if not need_ctx:
        return None, m_lat
    m_ctx = (g_c * (hc_f + hc_b).astype(g_c.dtype)) @ w_out
    return m_ctx, m_lat


def moe(h, router_w, router_bias, w_gate, w_up, w_down, s_gate, s_up, s_down):
    bsz, n, d = h.shape
    t = h.reshape(bsz * n, d)
    scores = jax.nn.sigmoid((t @ router_w).astype(F32))
    sel = scores + router_bias.astype(F32)
    grp_score = lax.top_k(sel.reshape(-1, N_GROUPS, EPG), 2)[0].sum(-1)
    _, gidx = lax.top_k(grp_score, TOPK_GROUPS)
    gmask = jax.nn.one_hot(gidx, N_GROUPS, dtype=F32).sum(-2) > 0
    sel = jnp.where(jnp.repeat(gmask, EPG, axis=-1), sel, -jnp.inf)
    _, eidx = lax.top_k(sel, TOP_K)
    w = jnp.take_along_axis(scores, eidx, axis=-1)
    w = w / jnp.sum(w, -1, keepdims=True) * ROUTED_SCALE
    gates = jnp.einsum('tk,tke->te', w, jax.nn.one_hot(eidx, N_EXPERTS, dtype=F32)).astype(h.dtype)
    out = (jax.nn.silu(t @ s_gate) * (t @ s_up)) @ s_down
    for g in range(N_GROUPS):
        sl = slice(g * EPG, (g + 1) * EPG)
        hg = jnp.einsum('td,edf->tef', t, w_gate[sl])
        hu = jnp.einsum('td,edf->tef', t, w_up[sl])
        out = out + jnp.einsum('tef,efd->td', jax.nn.silu(hg) * hu * gates[:, sl, None], w_down[sl])
    return out.reshape(bsz, n, d)


def setup_inputs(seed: int = 0) -> dict:
    key = jax.random.key(seed)
    ks = iter(jax.random.split(key, 64))

    def nrm(shape, scale=1.0):
        return jax.random.normal(next(ks), shape, F32) * scale

    D = D_MODEL
    min_decay = math.log(1.0 / HY_DECAY_TARGET) / HY_SLOW_PCT
    max_decay = math.log(1.0 / HY_DECAY_TARGET) / HY_FAST_PCT
    decay_base = jnp.tile(jnp.linspace(min_decay, max_decay, HY_W, dtype=F32), HY_ORDER * 2)
    lam_s = jnp.sqrt(jax.random.uniform(next(ks), (N_ODD, 2, D_RNN), F32, 0.81, 0.998))
    return {
        'x': nrm((BATCH, SEQ, D)),
        'c': nrm((BATCH, D)),
        'ctx': nrm((BATCH, CTX_LEN, D)),
        'c_ctx': nrm((D,)),
        'w_mod': nrm((DEPTH, D, N_MOD * D), 0.5 * D ** -0.5),
        'b_mod': nrm((DEPTH, N_MOD * D), 0.02),
        'ln1_g': 1.0 + nrm((DEPTH, D), 0.02),
        'ln1_b': nrm((DEPTH, D), 0.02),
        'ln2_g': 1.0 + nrm((DEPTH, D), 0.02),
        'ln2_b': nrm((DEPTH, D), 0.02),
        'ab_w_in': nrm((N_EVEN, D, AB_IN), D ** -0.5),
        'ab_w_out': nrm((N_EVEN, AB_OUT, D), DN_BETA * AB_OUT ** -0.5),
        'hy_short_w': nrm((N_EVEN, HY_SHORT, 3 * HY_W), HY_SHORT ** -0.5),
        'hy_short_b': nrm((N_EVEN, 3 * HY_W), 0.02),
        'hy_fw1': nrm((N_EVEN, HY_EMB, HY_HID), HY_EMB ** -0.5),
        'hy_fb1': nrm((N_EVEN, HY_HID), 0.02),
        'hy_fw2': nrm((N_EVEN, HY_HID, HY_HID), HY_HID ** -0.5),
        'hy_fb2': nrm((N_EVEN, HY_HID), 0.02),
        'hy_fw3': nrm((N_EVEN, HY_HID, HY_HID), HY_HID ** -0.5),
        'hy_fb3': nrm((N_EVEN, HY_HID), 0.02),
        'hy_fw4': nrm((N_EVEN, HY_HID, HY_N_FILT), 0.05 * HY_HID ** -0.5),
        'hy_freq': 1.0 + nrm((N_EVEN, HY_HID), 0.1),
        'hy_decay': decay_base[None] * (1.0 + nrm((N_EVEN, HY_N_FILT), 0.05)),
        'hy_skip': nrm((N_EVEN, HY_ORDER, HY_W)),
        'da_lq1': nrm((N_EVEN, DA_HD), 0.1),
        'da_lk1': nrm((N_EVEN, DA_HD), 0.1),
        'da_lq2': nrm((N_EVEN, DA_HD), 0.1),
        'da_lk2': nrm((N_EVEN, DA_HD), 0.1),
        'da_subln_g': 1.0 + nrm((N_EVEN, DA_VD), 0.02),
        'rg_w_in': nrm((N_ODD, D, 2 * D_RNN), D ** -0.5),
        'rg_w_out': nrm((N_ODD, D_RNN, D), DN_BETA * D_RNN ** -0.5),
        'rg_conv_w': nrm((N_ODD, RG_CONV, D_RNN), RG_CONV ** -0.5),
        'rg_conv_b': nrm((N_ODD, D_RNN), 0.02),
        'rg_wa': nrm((N_ODD, 2, RG_BLOCKS, RG_BW, RG_BW), RG_BW ** -0.5),
        'rg_ba': nrm((N_ODD, 2, D_RNN), 0.02),
        'rg_wx': nrm((N_ODD, 2, RG_BLOCKS, RG_BW, RG_BW), RG_BW ** -0.5),
        'rg_bx': nrm((N_ODD, 2, D_RNN), 0.02),
        'rg_lambda': jnp.log(lam_s) - jnp.log1p(-lam_s),
        'router_w': nrm((DEPTH, D, N_EXPERTS), D ** -0.5),
        'router_bias': nrm((DEPTH, N_EXPERTS), 0.01),
        'ex_w_gate': nrm((DEPTH, N_EXPERTS, D, D_EXPERT), D ** -0.5),
        'ex_w_up': nrm((DEPTH, N_EXPERTS, D, D_EXPERT), D ** -0.5),
        'ex_w_down': nrm((DEPTH, N_EXPERTS, D_EXPERT, D), DN_BETA * D_EXPERT ** -0.5),
        'sh_w_gate': nrm((DEPTH, D, D_SHARED), D ** -0.5),
        'sh_w_up': nrm((DEPTH, D, D_SHARED), D ** -0.5),
        'sh_w_down': nrm((DEPTH, D_SHARED, D), DN_BETA * D_SHARED ** -0.5),
    }


def reference(x, c, ctx, c_ctx, w_mod, b_mod, ln1_g, ln1_b, ln2_g, ln2_b,
              ab_w_in, ab_w_out, hy_short_w, hy_short_b, hy_fw1, hy_fb1, hy_fw2, hy_fb2,
              hy_fw3, hy_fb3, hy_fw4, hy_freq, hy_decay, hy_skip,
              da_lq1, da_lk1, da_lq2, da_lk2, da_subln_g,
              rg_w_in, rg_w_out, rg_conv_w, rg_conv_b, rg_wa, rg_ba, rg_wx, rg_bx, rg_lambda,
              router_w, router_bias, ex_w_gate, ex_w_up, ex_w_down, sh_w_gate, sh_w_up, sh_w_down):
    xs, cs = x, ctx
    silu_c = jax.nn.silu(c)
    silu_cc = jax.nn.silu(c_ctx)
    for l in range(DEPTH):
        need_ctx = l < DEPTH - 1
        mod_l = (silu_c @ w_mod[l] + b_mod[l])[:, None, :]
        mod_c = (silu_cc @ w_mod[l] + b_mod[l])[None, None, :]
        sh_l, sc_l, g_l, shf_l, scf_l, gf_l = jnp.split(mod_l, N_MOD, axis=-1)
        sh_c, sc_c, g_c, shf_c, scf_c, gf_c = jnp.split(mod_c, N_MOD, axis=-1)
        hin_l = xs * (1.0 + sc_l) + sh_l
        hin_c = cs * (1.0 + sc_c) + sh_c
        if l % 2 == 0:
            e = l // 2
            filt_p = (hy_fw1[e], hy_fb1[e], hy_fw2[e], hy_fb2[e], hy_fw3[e], hy_fb3[e],
                      hy_fw4[e], hy_freq[e], hy_decay[e])
            lam_init = 0.8 - 0.6 * math.exp(-0.3 * l)
            m_c, m_l = ab_mixer(hin_c, hin_l, ab_w_in[e], ab_w_out[e], hy_short_w[e], hy_short_b[e],
                                filt_p, hy_skip[e], da_lq1[e], da_lk1[e], da_lq2[e], da_lk2[e],
                                da_subln_g[e], lam_init, need_ctx)
        else:
            o = l // 2
            m_c, m_l = rg_mixer(hin_c, hin_l, rg_w_in[o], rg_w_out[o], rg_conv_w[o], rg_conv_b[o],
                                rg_wa[o], rg_ba[o], rg_wx[o], rg_bx[o], rg_lambda[o], need_ctx)
        moe_p = (router_w[l], router_bias[l], ex_w_gate[l], ex_w_up[l], ex_w_down[l],
                 sh_w_gate[l], sh_w_up[l], sh_w_down[l])
        xs = layer_norm(DN_ALPHA * xs + g_l * m_l, ln1_g[l], ln1_b[l])
        xs = layer_norm(DN_ALPHA * xs + gf_l * moe(xs * (1.0 + scf_l) + shf_l, *moe_p), ln2_g[l], ln2_b[l])
        if need_ctx:
            cs = layer_norm(DN_ALPHA * cs + g_c * m_c, ln1_g[l], ln1_b[l])
            cs = layer_norm(DN_ALPHA * cs + gf_c * moe(cs * (1.0 + scf_c) + shf_c, *moe_p), ln2_g[l], ln2_b[l])
    return xs
```

```python
import functools
import math

import jax
import jax.numpy as jnp
from jax import lax
from jax.experimental import pallas as pl
from jax.experimental.pallas import tpu as pltpu

F32 = jnp.float32
BF16 = jnp.bfloat16

DEPTH = 4
GRID_W = 64
N_MOD = 6
LN_EPS = 1e-5
DN_ALPHA = (2.0 * DEPTH) ** 0.25

HY_ORDER = 2
HY_EMB = 33
HY_BANDS = (HY_EMB - 1) // 2
DA_HEADS = 4
DA_HD = 64
DA_VD = 2 * DA_HD
ROPE_BASE = 10000.0
ROPE_NF = DA_HD // 4
RG_BLOCKS = 8
RG_C = 8.0
N_EXPERTS = 64
N_GROUPS = 8
EPG = N_EXPERTS // N_GROUPS
TOPK_GROUPS = 4
TOP_K = 8
ROUTED_SCALE = 2.5

LANES = 128
VMEM_LIMIT = 56 * 1024 * 1024


def _cparams(sem):
    return pltpu.CompilerParams(dimension_semantics=sem, vmem_limit_bytes=VMEM_LIMIT)


def _dot(a, b):
    return jnp.dot(a, b, preferred_element_type=F32)


def _dot_nt(a, b):
    return lax.dot_general(a, b, (((1,), (1,)), ((), ())), preferred_element_type=F32)


def _silu(x):
    return x * jax.nn.sigmoid(x)


def _layer_norm(r, g, b):
    mu = jnp.mean(r, axis=-1, keepdims=True)
    d = r - mu
    var = jnp.mean(d * d, axis=-1, keepdims=True)
    return d * lax.rsqrt(var + LN_EPS) * g + b


def _mod_kernel(cc_ref, w_ref, b_ref, o_ref):
    s = _silu(cc_ref[...])
    o_ref[0] = _dot(s.astype(BF16), w_ref[0].astype(BF16)) + b_ref[0]


def modulation(cc, w_mod, b_mod):
    r, d = cc.shape
    n = w_mod.shape[-1]
    tn = 1536
    return pl.pallas_call(
        _mod_kernel,
        grid=(DEPTH, n // tn),
        in_specs=[pl.BlockSpec((r, d), lambda l, j: (0, 0)),
                  pl.BlockSpec((1, d, tn), lambda l, j: (l, 0, j)),
                  pl.BlockSpec((1, 1, tn), lambda l, j: (l, 0, j))],
        out_specs=pl.BlockSpec((1, r, tn), lambda l, j: (l, 0, j)),
        out_shape=jax.ShapeDtypeStruct((DEPTH, r, n), F32),
        compiler_params=_cparams(("arbitrary", "arbitrary")),
        name="modulation",
    )(cc, w_mod, b_mod.reshape(DEPTH, 1, n))


def _ab_in_kernel(x_ref, sc_ref, sh_ref, w_ref, o_ref, *, tn):
    h = (x_ref[0] * (1.0 + sc_ref[0]) + sh_ref[0]).astype(BF16)
    for j in range(w_ref.shape[1] // tn):
        o_ref[0, :, j * tn:(j + 1) * tn] = _dot(h, w_ref[:, j * tn:(j + 1) * tn]).astype(BF16)


def ab_in(x, sc, sh, w):
    bsz, n, d = x.shape
    nout = w.shape[1]
    tm = min(n, 512)
    return pl.pallas_call(
        functools.partial(_ab_in_kernel, tn=1024),
        grid=(bsz, n // tm),
        in_specs=[pl.BlockSpec((1, tm, d), lambda b, i: (b, i, 0)),
                  pl.BlockSpec((1, 1, d), lambda b, i: (b, 0, 0)),
                  pl.BlockSpec((1, 1, d), lambda b, i: (b, 0, 0)),
                  pl.BlockSpec((d, nout), lambda b, i: (0, 0))],
        out_specs=pl.BlockSpec((1, tm, nout), lambda b, i: (b, i, 0)),
        out_shape=jax.ShapeDtypeStruct((bsz, n, nout), BF16),
        compiler_params=_cparams(("arbitrary", "arbitrary")),
        name="ab_in",
    )(x, sc, sh, w)


def _filter_kernel(z_ref, w1_ref, b1_ref, w2_ref, b2_ref, w3_ref, b3_ref, w4_ref, fr_ref, dec_ref,
                   hs_ref, hd_ref):
    hp = lax.Precision.HIGHEST
    z = z_ref[...]
    fr = fr_ref[...]
    h = jnp.sin(fr * (jnp.dot(z, w1_ref[...], precision=hp, preferred_element_type=F32) + b1_ref[...]))
    h = jnp.sin(fr * (jnp.dot(h, w2_ref[...], precision=hp, preferred_element_type=F32) + b2_ref[...]))
    h = jnp.sin(fr * (jnp.dot(h, w3_ref[...], precision=hp, preferred_element_type=F32) + b3_ref[...]))
    t = z[:, 0:1]
    f = jnp.dot(h, w4_ref[...], precision=hp, preferred_element_type=F32) * jnp.exp(-t * jnp.abs(dec_ref[...]))
    w = f.shape[1] // (2 * HY_ORDER)
    for o in range(HY_ORDER):
        hf = f[:, (2 * o) * w:(2 * o + 1) * w]
        hb = f[:, (2 * o + 1) * w:(2 * o + 2) * w]
        hs_ref[o] = (hf + hb).astype(BF16)
        hd_ref[o] = (hb - hf).astype(BF16)


def hyena_filter_parts(n, w1, b1, w2, b2, w3, b3, w4, freq, decay):
    t = jnp.linspace(0.0, 1.0, n, dtype=F32)[:, None]
    f = jnp.linspace(1e-4, HY_BANDS - 1, HY_BANDS, dtype=F32)
    ang = (2.0 * math.pi / n) * jnp.arange(n, dtype=F32)[:, None] * f[None, :]
    z = jnp.concatenate([t, jnp.cos(ang), -jnp.sin(ang)], axis=-1)
    z = jnp.pad(z, ((0, 0), (0, LANES - HY_EMB)))
    w1p = jnp.pad(w1, ((0, LANES - HY_EMB), (0, 0)))
    hid = w1.shape[1]
    nf = w4.shape[1]
    w = nf // (2 * HY_ORDER)
    tm = min(n, 512)
    full = lambda shape: pl.BlockSpec(shape, lambda i: (0,) * len(shape))
    return pl.pallas_call(
        _filter_kernel,
        grid=(n // tm,),
        in_specs=[pl.BlockSpec((tm, LANES), lambda i: (i, 0)),
                  full((LANES, hid)), full((1, hid)), full((hid, hid)), full((1, hid)),
                  full((hid, hid)), full((1, hid)), full((hid, nf)), full((1, hid)), full((1, nf))],
        out_specs=[pl.BlockSpec((HY_ORDER, tm, w), lambda i: (0, i, 0)),
                   pl.BlockSpec((HY_ORDER, tm, w), lambda i: (0, i, 0))],
        out_shape=[jax.ShapeDtypeStruct((HY_ORDER, n, w), BF16)] * 2,
        compiler_params=_cparams(("arbitrary",)),
        name="hyena_filter",
    )(z, w1p, b1[None], w2, b2[None], w3, b3[None], w4, freq[None], decay[None])


def dft_tables(n):
    k = (jnp.arange(n, dtype=jnp.int32)[:, None] * jnp.arange(n, dtype=jnp.int32)[None, :]) % (2 * n)
    ang = k.astype(F32) * (math.pi / n)
    c = jnp.cos(ang)
    s = jnp.sin(ang)
    alt = jnp.where(jnp.arange(n) % 2 == 0, 1.0, -1.0).astype(F32)
    s_fwd = s.at[0, :].set(alt)
    s_inv = s.at[:, 0].set(alt)
    return c.astype(BF16), s_fwd.astype(BF16), s_inv.astype(BF16)


def _spectrum_kernel(hs_ref, hd_ref, c_ref, s_ref, o_ref, *, n):
    f0 = pl.program_id(1) * c_ref.shape[0]
    hs = hs_ref[0]
    hr = _dot(c_ref[...], hs)
    hq = _dot(s_ref[...], hs)
    hi = _dot(s_ref[...], hd_ref[0])
    row = lax.broadcasted_iota(jnp.int32, hr.shape, 0) + f0
    dc = row == 0
    inv = 1.0 / (2 * n)
    o_ref[0, 0] = jnp.where(dc, hr * inv, hr * (2.0 * inv))
    o_ref[0, 1] = jnp.where(dc, 0.0, hi * (2.0 * inv))
    o_ref[0, 2] = jnp.where(dc, hq * inv, hr * (2.0 * inv))


def filter_spectrum(hs, hd, c, s_fwd):
    _, n, w = hs.shape
    tf = min(n, 256)
    return pl.pallas_call(
        functools.partial(_spectrum_kernel, n=n),
        grid=(HY_ORDER, n // tf),
        in_specs=[pl.BlockSpec((1, n, w), lambda o, f: (o, 0, 0)),
                  pl.BlockSpec((1, n, w), lambda o, f: (o, 0, 0)),
                  pl.BlockSpec((tf, n), lambda o, f: (f, 0)),
                  pl.BlockSpec((tf, n), lambda o, f: (f, 0))],
        out_specs=pl.BlockSpec((1, 3, tf, w), lambda o, f: (o, 0, f, 0)),
        out_shape=jax.ShapeDtypeStruct((HY_ORDER, 3, n, w), F32),
        compiler_params=_cparams(("arbitrary", "arbitrary")),
        name="filter_spectrum",
    )(hs, hd, c, s_fwd)


def _shift_rows(u, k):
    n = u.shape[0]
    row = lax.broadcasted_iota(jnp.int32, u.shape, 0)
    r = pltpu.roll(u, k % n, 0)
    if k > 0:
        return jnp.where(row < k, 0.0, r)
    return jnp.where(row >= n + k, 0.0, r)


def _hyena_kernel(p_ref, sw_ref, sb_ref, skip_ref, c_ref, s_ref, ct_ref, st_ref, tab_ref, o_ref,
                  sig_ref, z_ref, acc_ref, *, w):
    ph = pl.program_id(1)
    f = pl.program_id(2)
    last = pl.num_programs(2) - 1

    def short_conv(lo):
        u = p_ref[0, :, lo:lo + w].astype(F32)
        cw = sw_ref[:, lo:lo + w]
        return (cw[0:1] * _shift_rows(u, 1) + cw[1:2] * u + cw[2:3] * _shift_rows(u, -1)
                + sb_ref[:, lo:lo + w])

    @pl.when((ph == 0) & (f == 0))
    def _():
        sig_ref[...] = short_conv(0).astype(BF16)

    @pl.when(f == 0)
    def _():
        acc_ref[...] = jnp.zeros_like(acc_ref)

    sig = sig_ref[...]
    p = _dot(c_ref[...], sig)
    q = _dot(s_ref[...], sig)
    hr = tab_ref[0, 0]
    hi = tab_ref[0, 1]
    hq = tab_ref[0, 2]
    zr = (p * hr + q * hi).astype(BF16)
    zq = (q * hq - p * hi).astype(BF16)
    acc_ref[...] += _dot(ct_ref[...], zr) + _dot(st_ref[...], zq)

    @pl.when((f == last) & (ph == 0))
    def _():
        z = short_conv(w) * (acc_ref[...] + skip_ref[0:1] * short_conv(0))
        z_ref[...] = z
        sig_ref[...] = z.astype(BF16)

    @pl.when((f == last) & (ph == 1))
    def _():
        o_ref[0] = (short_conv(2 * w) * (acc_ref[...] + skip_ref[1:2] * z_ref[...])).astype(BF16)


def hyena_mixer(p, short_w, short_b, skip, tabs, c, s_fwd, s_inv):
    bsz, n, _ = p.shape
    w = skip.shape[1]
    tf = min(n, 256)
    return pl.pallas_call(
        functools.partial(_hyena_kernel, w=w),
        grid=(bsz, HY_ORDER, n // tf),
        in_specs=[pl.BlockSpec((1, n, 3 * w), lambda b, o, f: (b, 0, 0)),
                  pl.BlockSpec((3, 3 * w), lambda b, o, f: (0, 0)),
                  pl.BlockSpec((1, 3 * w), lambda b, o, f: (0, 0)),
                  pl.BlockSpec((HY_ORDER, w), lambda b, o, f: (0, 0)),
                  pl.BlockSpec((tf, n), lambda b, o, f: (f, 0)),
                  pl.BlockSpec((tf, n), lambda b, o, f: (f, 0)),
                  pl.BlockSpec((n, tf), lambda b, o, f: (0, f)),
                  pl.BlockSpec((n, tf), lambda b, o, f: (0, f)),
                  pl.BlockSpec((1, 3, tf, w), lambda b, o, f: (o, 0, f, 0))],
        out_specs=pl.BlockSpec((1, n, w), lambda b, o, f: (b, 0, 0)),
        out_shape=jax.ShapeDtypeStruct((bsz, n, w), BF16),
        scratch_shapes=[pltpu.VMEM((n, w), BF16), pltpu.VMEM((n, w), F32), pltpu.VMEM((n, w), F32)],
        compiler_params=_cparams(("arbitrary", "arbitrary", "arbitrary")),
        name="hyena_mixer",
    )(p, short_w, short_b[None], skip, c, s_fwd, c, s_inv, tabs)


def rope_tables(n):
    pos = jnp.arange(n)
    rc = jnp.stack([(pos // GRID_W).astype(F32), (pos % GRID_W).astype(F32)], axis=1)
    inv = ROPE_BASE ** (-jnp.arange(ROPE_NF, dtype=F32) / ROPE_NF)
    ang = rc[:, :, None] * inv
    cos = jnp.cos(ang)
    sin = jnp.sin(ang)
    zero = jnp.zeros_like(sin)
    tile = lambda a, b: jnp.tile(jnp.stack([a, b], axis=2).reshape(n, 4 * ROPE_NF), (1, 2))
    return tile(cos, cos), tile(-sin, zero), tile(zero, sin)


def _rope(x, cos, sin_a, sin_b):
    return (x * cos + pltpu.roll(x, LANES - ROPE_NF, 1) * sin_a + pltpu.roll(x, ROPE_NF, 1) * sin_b)


def _attn_kernel(*refs, n_pre, rope, lam_init, tq):
    if n_pre:
        q_ref, k_ref, v_ref, kc_ref, vc_ref, cos_ref, sa_ref, sb_ref, lam_ref, g_ref, o_ref, kbuf = refs
    else:
        q_ref, k_ref, v_ref, cos_ref, sa_ref, sb_ref, lam_ref, g_ref, o_ref, kbuf = refs
    n = q_ref.shape[1]
    lp = lam_ref[...]
    lam = (jnp.exp(jnp.sum(lp[0:1] * lp[1:2], axis=1, keepdims=True))
           - jnp.exp(jnp.sum(lp[2:3] * lp[3:4], axis=1, keepdims=True)) + lam_init)
    k = k_ref[0].astype(F32)
    if rope:
        k = _rope(k, cos_ref[...], sa_ref[...], sb_ref[...])
    kbuf[...] = k.astype(BF16)
    gain = g_ref[...] * (1.0 - lam_init)

    def body(i, carry):
        r0 = pl.multiple_of(i * tq, tq)
        q = q_ref[0, pl.ds(r0, tq), :].astype(F32)
        if rope:
            q = _rope(q, cos_ref[pl.ds(r0, tq), :], sa_ref[pl.ds(r0, tq), :], sb_ref[pl.ds(r0, tq), :])
        q = q * (DA_HD ** -0.5)
        lane = lax.broadcasted_iota(jnp.int32, q.shape, 1)
        es, invs = [], []
        for sub in range(2):
            qm = jnp.where((lane >= DA_HD) == bool(sub), q, 0.0).astype(BF16)
            sl = _dot_nt(qm, kbuf[...])
            m = jnp.max(sl, axis=-1, keepdims=True)
            if n_pre:
                sc = _dot_nt(qm, kc_ref[0])
                m = jnp.maximum(m, jnp.max(sc, axis=-1, keepdims=True))
                ec = jnp.exp(sc - m)
            el = jnp.exp(sl - m)
            tot = jnp.sum(el, axis=-1, keepdims=True)
            if n_pre:
                tot = tot + jnp.sum(ec, axis=-1, keepdims=True)
            es.append((ec if n_pre else None, el))
            invs.append(1.0 / tot)
        c0 = invs[0]
        c1 = lam * invs[1]
        wl = (es[0][1] * c0 - es[1][1] * c1).astype(BF16)
        o = _dot(wl, v_ref[0])
        if n_pre:
            wc = (es[0][0] * c0 - es[1][0] * c1).astype(BF16)
            o = o + _dot(wc, vc_ref[0])
        o = o * lax.rsqrt(jnp.mean(o * o, axis=-1, keepdims=True) + LN_EPS) * gain
        o_ref[0, pl.ds(r0, tq), :] = o.astype(BF16)
        return carry

    lax.fori_loop(0, n // tq, body, 0)


def diff_attention(p, p_pre, tables, lam_params, subln_g, lam_init, rope):
    bsz, n, width = p.shape
    qc = (width // 2) // LANES
    nh = DA_HEADS
    n_pre = 0 if p_pre is None else p_pre.shape[1]
    tq = min(n, 256)
    col = lambda off: pl.BlockSpec((1, n, LANES), lambda b, h, off=off: (b, 0, qc + off + h))
    in_specs = [col(0), col(nh), col(2 * nh)]
    args = [p, p, p]
    if n_pre:
        pre = lambda off: pl.BlockSpec((1, n_pre, LANES), lambda b, h, off=off: (b, 0, qc + off + h))
        in_specs += [pre(nh), pre(2 * nh)]
        args += [p_pre, p_pre]
    full = lambda shape: pl.BlockSpec(shape, lambda b, h: (0,) * len(shape))
    in_specs += [full((n, LANES))] * 3 + [full((4, LANES)), full((1, LANES))]
    args += list(tables) + [lam_params, subln_g[None]]
    return pl.pallas_call(
        functools.partial(_attn_kernel, n_pre=n_pre, rope=rope, lam_init=lam_init, tq=tq),
        grid=(bsz, nh),
        in_specs=in_specs,
        out_specs=pl.BlockSpec((1, n, LANES), lambda b, h: (b, 0, h)),
        out_shape=jax.ShapeDtypeStruct((bsz, n, nh * LANES), BF16),
        scratch_shapes=[pltpu.VMEM((n, LANES), BF16)],
        compiler_params=_cparams(("arbitrary", "arbitrary")),
        name="diff_attention",
    )(*args)


def _ab_out_kernel(a_ref, b_ref, w_ref, x_ref, g_ref, lg_ref, lb_ref, o_ref):
    ka = a_ref.shape[2]
    m = _dot(a_ref[0], w_ref[0:ka, :]) + _dot(b_ref[0], w_ref[ka:, :])
    o_ref[0] = _layer_norm(DN_ALPHA * x_ref[0] + g_ref[0] * m, lg_ref[...], lb_ref[...])


def _rg_out_kernel(gate_ref, h_ref, w_ref, x_ref, g_ref, lg_ref, lb_ref, o_ref):
    a = (gate_ref[0].astype(F32) * h_ref[0]).astype(BF16)
    m = _dot(a, w_ref[...])
    o_ref[0] = _layer_norm(DN_ALPHA * x_ref[0] + g_ref[0] * m, lg_ref[...], lb_ref[...])


def out_proj_norm(kern, a, b, w, x, g, ln_g, ln_b, name):
    bsz, n, d = x.shape
    tm = min(n, 512)
    row = lambda arr: pl.BlockSpec((1, tm, arr.shape[2]), lambda bb, i: (bb, i, 0))
    vec = pl.BlockSpec((1, d), lambda bb, i: (0, 0))
    return pl.pallas_call(
        kern,
        grid=(bsz, n // tm),
        in_specs=[row(a), row(b), pl.BlockSpec(w.shape, lambda bb, i: (0, 0)), row(x),
                  pl.BlockSpec((1, 1, d), lambda bb, i: (bb, 0, 0)), vec, vec],
        out_specs=row(x),
        out_shape=jax.ShapeDtypeStruct(x.shape, F32),
        compiler_params=_cparams(("arbitrary", "arbitrary")),
        name=name,
    )(a, b, w, x, g, ln_g[None], ln_b[None])


def _rg_in_kernel(x_ref, sc_ref, sh_ref, w_ref, gate_ref, rec_ref):
    d = x_ref.shape[2]
    h = (x_ref[0] * (1.0 + sc_ref[0]) + sh_ref[0]).astype(BF16)
    y = _dot(h, w_ref[:, 0:d])
    gelu = 0.5 * y * (1.0 + jnp.tanh(math.sqrt(2.0 / math.pi) * (y + 0.044715 * (y * y * y))))
    gate_ref[0] = gelu.astype(BF16)
    rec_ref[0] = _dot(h, w_ref[:, d:])


def rg_in(x, sc, sh, w):
    bsz, n, d = x.shape
    tm = min(n, 512)
    row = pl.BlockSpec((1, tm, d), lambda b, i: (b, i, 0))
    vec = pl.BlockSpec((1, 1, d), lambda b, i: (b, 0, 0))
    return pl.pallas_call(
        _rg_in_kernel,
        grid=(bsz, n // tm),
        in_specs=[row, vec, vec, pl.BlockSpec(w.shape, lambda b, i: (0, 0))],
        out_specs=[row, row],
        out_shape=[jax.ShapeDtypeStruct((bsz, n, d), BF16), jax.ShapeDtypeStruct((bsz, n, d), F32)],
        compiler_params=_cparams(("arbitrary", "arbitrary")),
        name="rg_in",
    )(x, sc, sh, w)


def _rg_scan_kernel(rc_ref, rl_ref, cw_ref, cb_ref, wa_ref, wx_ref, ba_ref, bx_ref, lam_ref,
                    hc_ref, hl_ref, uc_ref, ul_ref, a_ref, b_ref, *, chunk):
    d = rc_ref.shape[2]
    bw = d // RG_BLOCKS

    def conv(r):
        cw = cw_ref[...]
        return (cw[0:1] * _shift_rows(r, 2) + cw[1:2] * _shift_rows(r, 1) + cw[2:3] * r
                + cw[3:4] * _shift_rows(r, -1) + cb_ref[...])

    uc_ref[...] = conv(rc_ref[0])
    ul_ref[...] = conv(rl_ref[0])

    def coeffs(u_ref, r0, dr):
        u = u_ref[pl.ds(r0, chunk), :]
        ub = u.astype(BF16)
        lam = lam_ref[dr:dr + 1, :]
        log_sig = jnp.minimum(lam, 0.0) - jnp.log(1.0 + jnp.exp(-jnp.abs(lam)))
        for kb in range(RG_BLOCKS):
            sl = slice(kb * bw, (kb + 1) * bw)
            r = jax.nn.sigmoid(_dot(ub[:, sl], wa_ref[dr, kb]) + ba_ref[dr:dr + 1, sl])
            i = jax.nn.sigmoid(_dot(ub[:, sl], wx_ref[dr, kb]) + bx_ref[dr:dr + 1, sl])
            log_a = RG_C * r * log_sig[:, sl]
            a = jnp.exp(log_a)
            a_ref[:, sl] = a
            b_ref[:, sl] = jnp.sqrt(1.0 - a * a) * (i * u[:, sl])

    def run(u_ref, o_ref, h, dr):
        n = u_ref.shape[0]
        nchunk = n // chunk

        def chunk_body(ci, h):
            c = ci if dr == 0 else nchunk - 1 - ci
            r0 = pl.multiple_of(c * chunk, chunk)
            coeffs(u_ref, r0, dr)

            def row_body(ti, h):
                t = ti if dr == 0 else chunk - 1 - ti
                h = a_ref[pl.ds(t, 1), :] * h + b_ref[pl.ds(t, 1), :]
                if dr == 0:
                    o_ref[0, pl.ds(r0 + t, 1), :] = h
                else:
                    o_ref[0, pl.ds(r0 + t, 1), :] += h
                return h

            return lax.fori_loop(0, chunk, row_body, h)

        return lax.fori_loop(0, nchunk, chunk_body, h)

    for dr in range(2):
        h = jnp.zeros((1, d), F32)
        h = run(uc_ref, hc_ref, h, dr)
        run(ul_ref, hl_ref, h, dr)


def rg_scan(rec_c, rec_l, conv_w, conv_b, wa, wx, ba, bx, lam):
    bsz, nc, d = rec_c.shape
    nl = rec_l.shape[1]
    chunk = 256
    full = lambda arr: pl.BlockSpec(arr.shape, lambda b: (0,) * arr.ndim)
    seq = lambda n: pl.BlockSpec((1, n, d), lambda b: (b, 0, 0), pipeline_mode=pl.Buffered(1))
    cb = conv_b[None]
    return pl.pallas_call(
        functools.partial(_rg_scan_kernel, chunk=chunk),
        grid=(bsz,),
        in_specs=[seq(nc), seq(nl), full(conv_w), full(cb), full(wa), full(wx), full(ba), full(bx),
                  full(lam)],
        out_specs=[seq(nc), seq(nl)],
        out_shape=[jax.ShapeDtypeStruct((bsz, nc, d), F32), jax.ShapeDtypeStruct((bsz, nl, d), F32)],
        scratch_shapes=[pltpu.VMEM((nc, d), F32), pltpu.VMEM((nl, d), F32),
                        pltpu.VMEM((chunk, d), F32), pltpu.VMEM((chunk, d), F32)],
        compiler_params=_cparams(("arbitrary",)),
        name="rg_scan",
    )(rec_c, rec_l, conv_w, cb, wa, wx, ba, bx, lam)


def _route_kernel(x_ref, sc_ref, sh_ref, wh_ref, wl_ref, bias_ref, o_ref):
    t = x_ref[0] * (1.0 + sc_ref[0]) + sh_ref[0]
    th = t.astype(BF16)
    tl = (t - th.astype(F32)).astype(BF16)
    logits = _dot_nt(wh_ref[...], th) + (_dot_nt(wh_ref[...], tl) + _dot_nt(wl_ref[...], th))
    scores = jax.nn.sigmoid(logits)
    sel = scores + bias_ref[...]
    ninf = -jnp.inf
    sub = lax.broadcasted_iota(jnp.int32, (EPG, sel.shape[1]), 0)
    gscore = []
    for g in range(N_GROUPS):
        slab = sel[g * EPG:(g + 1) * EPG]
        m1 = jnp.max(slab, axis=0, keepdims=True)
        first = jnp.min(jnp.where(slab == m1, sub, EPG), axis=0, keepdims=True)
        m2 = jnp.max(jnp.where(sub == first, ninf, slab), axis=0, keepdims=True)
        gscore.append(m1 + m2)
    slabs = []
    for g in range(N_GROUPS):
        beaten = jnp.zeros(gscore[g].shape, F32)
        for o in range(N_GROUPS):
            if o != g:
                wins = (gscore[o] >= gscore[g]) if o < g else (gscore[o] > gscore[g])
                beaten = beaten + jnp.where(wins, 1.0, 0.0)
        keep = jnp.broadcast_to(beaten < TOPK_GROUPS, (EPG, sel.shape[1]))
        slabs.append(jnp.where(keep, sel[g * EPG:(g + 1) * EPG], ninf))
    v = jnp.concatenate(slabs, axis=0)
    eidx = lax.broadcasted_iota(jnp.int32, v.shape, 0)
    w = jnp.zeros(v.shape, F32)
    for _ in range(TOP_K):
        m = jnp.max(v, axis=0, keepdims=True)
        first = jnp.min(jnp.where(v == m, eidx, N_EXPERTS), axis=0, keepdims=True)
        hit = eidx == first
        w = jnp.where(hit, scores, w)
        v = jnp.where(hit, ninf, v)
    gates = w / jnp.sum(w, axis=0, keepdims=True) * ROUTED_SCALE
    gates = jnp.concatenate([gates, jnp.zeros((LANES - N_EXPERTS, gates.shape[1]), F32)], axis=0)
    o_ref[...] = gates.T


def route(x, sc, sh, rw_hi, rw_lo, bias):
    bsz, n, d = x.shape
    tm = min(n, 512)
    nt = n // tm
    full = lambda arr: pl.BlockSpec(arr.shape, lambda b, i: (0,) * arr.ndim)
    vec = pl.BlockSpec((1, 1, d), lambda b, i: (b, 0, 0))
    return pl.pallas_call(
        _route_kernel,
        grid=(bsz, nt),
        in_specs=[pl.BlockSpec((1, tm, d), lambda b, i: (b, i, 0)), vec, vec, full(rw_hi), full(rw_lo),
                  full(bias)],
        out_specs=pl.BlockSpec((tm, LANES), lambda b, i: (b * nt + i, 0)),
        out_shape=jax.ShapeDtypeStruct((bsz * n, LANES), F32),
        compiler_params=_cparams(("arbitrary", "arbitrary")),
        name="route",
    )(x, sc, sh, rw_hi, rw_lo, bias)


def _moe_dense_kernel(x_ref, sc_ref, sh_ref, gf_ref, gates_ref, wg_ref, wu_ref, wd_ref, sg_ref, su_ref,
                      sd_ref, lg_ref, lb_ref, o_ref, t_ref, acc_ref, *, epb):
    e = pl.program_id(2)

    @pl.when(e == 0)
    def _():
        t = (x_ref[0] * (1.0 + sc_ref[0]) + sh_ref[0]).astype(BF16)
        t_ref[...] = t
        hs = _silu(_dot(t, sg_ref[0].astype(BF16))) * _dot(t, su_ref[0].astype(BF16))
        acc_ref[...] = _dot(hs.astype(BF16), sd_ref[0].astype(BF16))

    t = t_ref[...]
    gates = gates_ref[...]
    lane = lax.broadcasted_iota(jnp.int32, gates.shape, 1)
    for j in range(epb):
        gcol = jnp.sum(jnp.where(lane == e * epb + j, gates, 0.0), axis=1, keepdims=True)
        h = _silu(_dot(t, wg_ref[0, j].astype(BF16))) * _dot(t, wu_ref[0, j].astype(BF16)) * gcol
        acc_ref[...] += _dot(h.astype(BF16), wd_ref[0, j].astype(BF16))

    @pl.when(e == pl.num_programs(2) - 1)
    def _():
        o_ref[0] = _layer_norm(DN_ALPHA * x_ref[0] + gf_ref[0] * acc_ref[...], lg_ref[...], lb_ref[...])


def moe_dense(x, sc, sh, gf, gates, layer, wg, wu, wd, sg, su, sd, ln_g, ln_b):
    bsz, n, d = x.shape
    de = wg.shape[3]
    tm = min(n, 1024)
    nt = n // tm
    epb = 2
    row = pl.BlockSpec((1, tm, d), lambda b, i, e: (b, i, 0))
    vec = pl.BlockSpec((1, 1, d), lambda b, i, e: (b, 0, 0))
    lnv = pl.BlockSpec((1, d), lambda b, i, e: (0, 0))
    return pl.pallas_call(
        functools.partial(_moe_dense_kernel, epb=epb),
        grid=(bsz, nt, N_EXPERTS // epb),
        in_specs=[row, vec, vec, vec,
                  pl.BlockSpec((tm, LANES), lambda b, i, e: (b * nt + i, 0)),
                  pl.BlockSpec((1, epb, d, de), lambda b, i, e: (layer, e, 0, 0)),
                  pl.BlockSpec((1, epb, d, de), lambda b, i, e: (layer, e, 0, 0)),
                  pl.BlockSpec((1, epb, de, d), lambda b, i, e: (layer, e, 0, 0)),
                  pl.BlockSpec((1, d, sg.shape[2]), lambda b, i, e: (layer, 0, 0)),
                  pl.BlockSpec((1, d, su.shape[2]), lambda b, i, e: (layer, 0, 0)),
                  pl.BlockSpec((1, sd.shape[1], d), lambda b, i, e: (layer, 0, 0)),
                  lnv, lnv],
        out_specs=row,
        out_shape=jax.ShapeDtypeStruct(x.shape, F32),
        scratch_shapes=[pltpu.VMEM((tm, d), BF16), pltpu.VMEM((tm, d), F32)],
        compiler_params=_cparams(("arbitrary", "arbitrary", "arbitrary")),
        name="moe_dense",
    )(x, sc, sh, gf, gates, wg, wu, wd, sg, su, sd, ln_g[None], ln_b[None])


def kernel(x, c, ctx, c_ctx, w_mod, b_mod, ln1_g, ln1_b, ln2_g, ln2_b, ab_w_in, ab_w_out, hy_short_w, hy_short_b, hy_fw1, hy_fb1, hy_fw2, hy_fb2, hy_fw3, hy_fb3, hy_fw4, hy_freq, hy_decay, hy_skip, da_lq1, da_lk1, da_lq2, da_lk2, da_subln_g, rg_w_in, rg_w_out, rg_conv_w, rg_conv_b, rg_wa, rg_ba, rg_wx, rg_bx, rg_lambda, router_w, router_bias, ex_w_gate, ex_w_up, ex_w_down, sh_w_gate, sh_w_up, sh_w_down):
    return forward(dict(locals()))


def split_mod(mod_l, bsz, d):
    ml = [mod_l[:bsz, k * d:(k + 1) * d][:, None, :] for k in range(N_MOD)]
    mc = [jnp.broadcast_to(mod_l[bsz, k * d:(k + 1) * d][None, None, :], (bsz, 1, d)) for k in range(N_MOD)]
    return ml, mc


def mixer_even(p, l, xs, cs, ml, mc, need_ctx):
    e = l // 2
    n_lat, n_ctx = xs.shape[1], cs.shape[1]
    lam_init = 0.8 - 0.6 * math.exp(-0.3 * l)
    w_in = p["ab_w_in"][e].astype(BF16)
    w_out = p["ab_w_out"][e].astype(BF16)
    p_lat = ab_in(xs, ml[1], ml[0], w_in)
    p_ctx = ab_in(cs, mc[1], mc[0], w_in)
    filt_p = tuple(p[k][e] for k in ("hy_fw1", "hy_fb1", "hy_fw2", "hy_fb2", "hy_fw3", "hy_fb3", "hy_fw4",
                                     "hy_freq", "hy_decay"))
    lam_params = jnp.pad(jnp.stack([p["da_lq1"][e], p["da_lk1"][e], p["da_lq2"][e], p["da_lk2"][e]]),
                         ((0, 0), (0, LANES - DA_HD)))
    rope_lat = rope_tables(n_lat)
    rope_ctx = tuple(t[:n_ctx] for t in rope_lat)

    def hyena(proj, n):
        dft = dft_tables(n)
        hs, hd = hyena_filter_parts(n, *filt_p)
        tabs = filter_spectrum(hs, hd, dft[0], dft[1])
        return hyena_mixer(proj, p["hy_short_w"][e], p["hy_short_b"][e], p["hy_skip"][e], tabs, *dft)

    g1, b1 = p["ln1_g"][l], p["ln1_b"][l]
    sub_g = p["da_subln_g"][e]
    y_lat = hyena(p_lat, n_lat)
    o_lat = diff_attention(p_lat, p_ctx, rope_lat, lam_params, sub_g, lam_init, True)
    xs_new = out_proj_norm(_ab_out_kernel, y_lat, o_lat, w_out, xs, ml[2], g1, b1, "ab_out")
    if need_ctx:
        y_ctx = hyena(p_ctx, n_ctx)
        o_ctx = diff_attention(p_ctx, None, rope_ctx, lam_params, sub_g, lam_init, False)
        cs = out_proj_norm(_ab_out_kernel, y_ctx, o_ctx, w_out, cs, mc[2], g1, b1, "ab_out")
    return xs_new, cs


def mixer_odd(p, l, xs, cs, ml, mc, need_ctx):
    o = l // 2
    w_in = p["rg_w_in"][o].astype(BF16)
    w_out = p["rg_w_out"][o].astype(BF16)
    gate_l, rec_l = rg_in(xs, ml[1], ml[0], w_in)
    gate_c, rec_c = rg_in(cs, mc[1], mc[0], w_in)
    hc, hl = rg_scan(rec_c, rec_l, p["rg_conv_w"][o], p["rg_conv_b"][o], p["rg_wa"][o].astype(BF16),
                     p["rg_wx"][o].astype(BF16), p["rg_ba"][o], p["rg_bx"][o], p["rg_lambda"][o])
    g1, b1 = p["ln1_g"][l], p["ln1_b"][l]
    xs = out_proj_norm(_rg_out_kernel, gate_l, hl, w_out, xs, ml[2], g1, b1, "rg_out")
    if need_ctx:
        cs = out_proj_norm(_rg_out_kernel, gate_c, hc, w_out, cs, mc[2], g1, b1, "rg_out")
    return xs, cs


def moe_block(p, l, h, m):
    rw_t = p["router_w"][l].T
    rw_hi = rw_t.astype(BF16)
    rw_lo = (rw_t - rw_hi.astype(F32)).astype(BF16)
    gates = route(h, m[4], m[3], rw_hi, rw_lo, p["router_bias"][l][:, None])
    return moe_dense(h, m[4], m[3], m[5], gates, l, p["ex_w_gate"], p["ex_w_up"], p["ex_w_down"],
                     p["sh_w_gate"], p["sh_w_up"], p["sh_w_down"], p["ln2_g"][l], p["ln2_b"][l])


def run_layer(p, l, xs, cs, mod_l):
    bsz, _, d = xs.shape
    need_ctx = l < DEPTH - 1
    ml, mc = split_mod(mod_l, bsz, d)
    mixer = mixer_even if l % 2 == 0 else mixer_odd
    xs, cs = mixer(p, l, xs, cs, ml, mc, need_ctx)
    xs = moe_block(p, l, xs, ml)
    if need_ctx:
        cs = moe_block(p, l, cs, mc)
    return xs, cs


def conditioning_rows(c, c_ctx):
    bsz, d = c.shape
    rows = 8 * ((bsz + 1 + 7) // 8)
    return jnp.zeros((rows, d), F32).at[:bsz].set(c).at[bsz].set(c_ctx)


def forward(p):
    mod = modulation(conditioning_rows(p["c"], p["c_ctx"]), p["w_mod"], p["b_mod"])
    xs, cs = p["x"], p["ctx"]
    for l in range(DEPTH):
        xs, cs = run_layer(p, l, xs, cs, mod[l])
    return xs
```

```python
import functools
import math

import jax
import jax.numpy as jnp
from jax import lax
from jax.experimental import pallas as pl
from jax.experimental.pallas import tpu as pltpu

F32 = jnp.float32
BF16 = jnp.bfloat16

DEPTH = 4
GRID_W = 64
N_MOD = 6
LN_EPS = 1e-5
DN_ALPHA = (2.0 * DEPTH) ** 0.25

HY_ORDER = 2
HY_EMB = 33
HY_BANDS = (HY_EMB - 1) // 2
DA_HEADS = 4
DA_HD = 64
DA_VD = 2 * DA_HD
ROPE_BASE = 10000.0
ROPE_NF = DA_HD // 4
RG_BLOCKS = 8
RG_C = 8.0
N_EXPERTS = 64
N_GROUPS = 8
EPG = N_EXPERTS // N_GROUPS
TOPK_GROUPS = 4
TOP_K = 8
ROUTED_SCALE = 2.5

LANES = 128
VMEM_LIMIT = 56 * 1024 * 1024


def _cparams(sem):
    return pltpu.CompilerParams(dimension_semantics=sem, vmem_limit_bytes=VMEM_LIMIT)


def _dot(a, b):
    return jnp.dot(a, b, preferred_element_type=F32)


def _dot_nt(a, b):
    return lax.dot_general(a, b, (((1,), (1,)), ((), ())), preferred_element_type=F32)


def _silu(x):
    return x * jax.nn.sigmoid(x)


def _layer_norm(r, g, b):
    mu = jnp.mean(r, axis=-1, keepdims=True)
    d = r - mu
    var = jnp.mean(d * d, axis=-1, keepdims=True)
    return d * lax.rsqrt(var + LN_EPS) * g + b


def _mod_kernel(cc_ref, w_ref, b_ref, o_ref):
    s = _silu(cc_ref[...])
    o_ref[0] = _dot(s.astype(BF16), w_ref[0].astype(BF16)) + b_ref[0]


def modulation(cc, w_mod, b_mod):
    r, d = cc.shape
    n = w_mod.shape[-1]
    tn = 1536
    return pl.pallas_call(
        _mod_kernel,
        grid=(DEPTH, n // tn),
        in_specs=[pl.BlockSpec((r, d), lambda l, j: (0, 0)),
                  pl.BlockSpec((1, d, tn), lambda l, j: (l, 0, j)),
                  pl.BlockSpec((1, 1, tn), lambda l, j: (l, 0, j))],
        out_specs=pl.BlockSpec((1, r, tn), lambda l, j: (l, 0, j)),
        out_shape=jax.ShapeDtypeStruct((DEPTH, r, n), F32),
        compiler_params=_cparams(("arbitrary", "arbitrary")),
        name="modulation",
    )(cc, w_mod, b_mod.reshape(DEPTH, 1, n))


def _ab_in_kernel(x_ref, sc_ref, sh_ref, w_ref, o_ref, *, tn):
    h = (x_ref[0] * (1.0 + sc_ref[0]) + sh_ref[0]).astype(BF16)
    for j in range(w_ref.shape[1] // tn):
        o_ref[0, :, j * tn:(j + 1) * tn] = _dot(h, w_ref[:, j * tn:(j + 1) * tn]).astype(BF16)


def ab_in(x, sc, sh, w):
    bsz, n, d = x.shape
    nout = w.shape[1]
    tm = min(n, 512)
    return pl.pallas_call(
        functools.partial(_ab_in_kernel, tn=1024),
        grid=(bsz, n // tm),
        in_specs=[pl.BlockSpec((1, tm, d), lambda b, i: (b, i, 0)),
                  pl.BlockSpec((1, 1, d), lambda b, i: (b, 0, 0)),
                  pl.BlockSpec((1, 1, d), lambda b, i: (b, 0, 0)),
                  pl.BlockSpec((d, nout), lambda b, i: (0, 0))],
        out_specs=pl.BlockSpec((1, tm, nout), lambda b, i: (b, i, 0)),
        out_shape=jax.ShapeDtypeStruct((bsz, n, nout), BF16),
        compiler_params=_cparams(("arbitrary", "arbitrary")),
        name="ab_in",
    )(x, sc, sh, w)


def _filter_kernel(z_ref, w1_ref, b1_ref, w2_ref, b2_ref, w3_ref, b3_ref, w4_ref, fr_ref, dec_ref,
                   hs_ref, hd_ref):
    hp = lax.Precision.HIGHEST
    z = z_ref[...]
    fr = fr_ref[...]
    h = jnp.sin(fr * (jnp.dot(z, w1_ref[...], precision=hp, preferred_element_type=F32) + b1_ref[...]))
    h = jnp.sin(fr * (jnp.dot(h, w2_ref[...], precision=hp, preferred_element_type=F32) + b2_ref[...]))
    h = jnp.sin(fr * (jnp.dot(h, w3_ref[...], precision=hp, preferred_element_type=F32) + b3_ref[...]))
    t = z[:, 0:1]
    f = jnp.dot(h, w4_ref[...], precision=hp, preferred_element_type=F32) * jnp.exp(-t * jnp.abs(dec_ref[...]))
    w = f.shape[1] // (2 * HY_ORDER)
    for o in range(HY_ORDER):
        hf = f[:, (2 * o) * w:(2 * o + 1) * w]
        hb = f[:, (2 * o + 1) * w:(2 * o + 2) * w]
        hs_ref[o] = (hf + hb).astype(BF16)
        hd_ref[o] = (hb - hf).astype(BF16)


def hyena_filter_parts(n, w1, b1, w2, b2, w3, b3, w4, freq, decay):
    t = jnp.linspace(0.0, 1.0, n, dtype=F32)[:, None]
    f = jnp.linspace(1e-4, HY_BANDS - 1, HY_BANDS, dtype=F32)
    ang = (2.0 * math.pi / n) * jnp.arange(n, dtype=F32)[:, None] * f[None, :]
    z = jnp.concatenate([t, jnp.cos(ang), -jnp.sin(ang)], axis=-1)
    z = jnp.pad(z, ((0, 0), (0, LANES - HY_EMB)))
    w1p = jnp.pad(w1, ((0, LANES - HY_EMB), (0, 0)))
    hid = w1.shape[1]
    nf = w4.shape[1]
    w = nf // (2 * HY_ORDER)
    tm = min(n, 512)
    full = lambda shape: pl.BlockSpec(shape, lambda i: (0,) * len(shape))
    return pl.pallas_call(
        _filter_kernel,
        grid=(n // tm,),
        in_specs=[pl.BlockSpec((tm, LANES), lambda i: (i, 0)),
                  full((LANES, hid)), full((1, hid)), full((hid, hid)), full((1, hid)),
                  full((hid, hid)), full((1, hid)), full((hid, nf)), full((1, hid)), full((1, nf))],
        out_specs=[pl.BlockSpec((HY_ORDER, tm, w), lambda i: (0, i, 0)),
                   pl.BlockSpec((HY_ORDER, tm, w), lambda i: (0, i, 0))],
        out_shape=[jax.ShapeDtypeStruct((HY_ORDER, n, w), BF16)] * 2,
        compiler_params=_cparams(("arbitrary",)),
        name="hyena_filter",
    )(z, w1p, b1[None], w2, b2[None], w3, b3[None], w4, freq[None], decay[None])


def dft_tables(n):
    k = (jnp.arange(n, dtype=jnp.int32)[:, None] * jnp.arange(n, dtype=jnp.int32)[None, :]) % (2 * n)
    ang = k.astype(F32) * (math.pi / n)
    c = jnp.cos(ang)
    s = jnp.sin(ang)
    alt = jnp.where(jnp.arange(n) % 2 == 0, 1.0, -1.0).astype(F32)
    s_fwd = s.at[0, :].set(alt)
    s_inv = s.at[:, 0].set(alt)
    return c.astype(BF16), s_fwd.astype(BF16), s_inv.astype(BF16)


def _spectrum_kernel(hs_ref, hd_ref, c_ref, s_ref, o_ref, *, n):
    f0 = pl.program_id(1) * c_ref.shape[0]
    hs = hs_ref[0]
    hr = _dot(c_ref[...], hs)
    hq = _dot(s_ref[...], hs)
    hi = _dot(s_ref[...], hd_ref[0])
    row = lax.broadcasted_iota(jnp.int32, hr.shape, 0) + f0
    dc = row == 0
    inv = 1.0 / (2 * n)
    o_ref[0, 0] = jnp.where(dc, hr * inv, hr * (2.0 * inv))
    o_ref[0, 1] = jnp.where(dc, 0.0, hi * (2.0 * inv))
    o_ref[0, 2] = jnp.where(dc, hq * inv, hr * (2.0 * inv))


def filter_spectrum(hs, hd, c, s_fwd):
    _, n, w = hs.shape
    tf = min(n, 256)
    return pl.pallas_call(
        functools.partial(_spectrum_kernel, n=n),
        grid=(HY_ORDER, n // tf),
        in_specs=[pl.BlockSpec((1, n, w), lambda o, f: (o, 0, 0)),
                  pl.BlockSpec((1, n, w), lambda o, f: (o, 0, 0)),
                  pl.BlockSpec((tf, n), lambda o, f: (f, 0)),
                  pl.BlockSpec((tf, n), lambda o, f: (f, 0))],
        out_specs=pl.BlockSpec((1, 3, tf, w), lambda o, f: (o, 0, f, 0)),
        out_shape=jax.ShapeDtypeStruct((HY_ORDER, 3, n, w), F32),
        compiler_params=_cparams(("arbitrary", "arbitrary")),
        name="filter_spectrum",
    )(hs, hd, c, s_fwd)


def _shift_rows(u, k):
    n = u.shape[0]
    row = lax.broadcasted_iota(jnp.int32, u.shape, 0)
    r = pltpu.roll(u, k % n, 0)
    if k > 0:
        return jnp.where(row < k, 0.0, r)
    return jnp.where(row >= n + k, 0.0, r)


def _hyena_kernel(p_ref, sw_ref, sb_ref, skip_ref, c_ref, s_ref, ct_ref, st_ref, tab_ref, o_ref,
                  sig_ref, z_ref, acc_ref, *, w):
    ph = pl.program_id(1)
    f = pl.program_id(2)
    last = pl.num_programs(2) - 1

    def short_conv(lo):
        u = p_ref[0, :, lo:lo + w].astype(F32)
        cw = sw_ref[:, lo:lo + w]
        return (cw[0:1] * _shift_rows(u, 1) + cw[1:2] * u + cw[2:3] * _shift_rows(u, -1)
                + sb_ref[:, lo:lo + w])

    @pl.when((ph == 0) & (f == 0))
    def _():
        sig_ref[...] = short_conv(0).astype(BF16)

    @pl.when(f == 0)
    def _():
        acc_ref[...] = jnp.zeros_like(acc_ref)

    sig = sig_ref[...]
    p = _dot(c_ref[...], sig)
    q = _dot(s_ref[...], sig)
    hr = tab_ref[0, 0]
    hi = tab_ref[0, 1]
    hq = tab_ref[0, 2]
    zr = (p * hr + q * hi).astype(BF16)
    zq = (q * hq - p * hi).astype(BF16)
    acc_ref[...] += _dot(ct_ref[...], zr) + _dot(st_ref[...], zq)

    @pl.when((f == last) & (ph == 0))
    def _():
        z = short_conv(w) * (acc_ref[...] + skip_ref[0:1] * short_conv(0))
        z_ref[...] = z
        sig_ref[...] = z.astype(BF16)

    @pl.when((f == last) & (ph == 1))
    def _():
        o_ref[0] = (short_conv(2 * w) * (acc_ref[...] + skip_ref[1:2] * z_ref[...])).astype(BF16)


def hyena_mixer(p, short_w, short_b, skip, tabs, c, s_fwd, s_inv):
    bsz, n, _ = p.shape
    w = skip.shape[1]
    tf = min(n, 256)
    return pl.pallas_call(
        functools.partial(_hyena_kernel, w=w),
        grid=(bsz, HY_ORDER, n // tf),
        in_specs=[pl.BlockSpec((1, n, 3 * w), lambda b, o, f: (b, 0, 0)),
                  pl.BlockSpec((3, 3 * w), lambda b, o, f: (0, 0)),
                  pl.BlockSpec((1, 3 * w), lambda b, o, f: (0, 0)),
                  pl.BlockSpec((HY_ORDER, w), lambda b, o, f: (0, 0)),
                  pl.BlockSpec((tf, n), lambda b, o, f: (f, 0)),
                  pl.BlockSpec((tf, n), lambda b, o, f: (f, 0)),
                  pl.BlockSpec((n, tf), lambda b, o, f: (0, f)),
                  pl.BlockSpec((n, tf), lambda b, o, f: (0, f)),
                  pl.BlockSpec((1, 3, tf, w), lambda b, o, f: (o, 0, f, 0))],
        out_specs=pl.BlockSpec((1, n, w), lambda b, o, f: (b, 0, 0)),
        out_shape=jax.ShapeDtypeStruct((bsz, n, w), BF16),
        scratch_shapes=[pltpu.VMEM((n, w), BF16), pltpu.VMEM((n, w), F32), pltpu.VMEM((n, w), F32)],
        compiler_params=_cparams(("arbitrary", "arbitrary", "arbitrary")),
        name="hyena_mixer",
    )(p, short_w, short_b[None], skip, c, s_fwd, c, s_inv, tabs)


def rope_tables(n):
    pos = jnp.arange(n)
    rc = jnp.stack([(pos // GRID_W).astype(F32), (pos % GRID_W).astype(F32)], axis=1)
    inv = ROPE_BASE ** (-jnp.arange(ROPE_NF, dtype=F32) / ROPE_NF)
    ang = rc[:, :, None] * inv
    cos = jnp.cos(ang)
    sin = jnp.sin(ang)
    zero = jnp.zeros_like(sin)
    tile = lambda a, b: jnp.tile(jnp.stack([a, b], axis=2).reshape(n, 4 * ROPE_NF), (1, 2))
    return tile(cos, cos), tile(-sin, zero), tile(zero, sin)


def _rope(x, cos, sin_a, sin_b):
    return (x * cos + pltpu.roll(x, LANES - ROPE_NF, 1) * sin_a + pltpu.roll(x, ROPE_NF, 1) * sin_b)


def _attn_kernel(*refs, n_pre, rope, lam_init, tq):
    if n_pre:
        q_ref, k_ref, v_ref, kc_ref, vc_ref, cos_ref, sa_ref, sb_ref, lam_ref, g_ref, o_ref, kbuf = refs
    else:
        q_ref, k_ref, v_ref, cos_ref, sa_ref, sb_ref, lam_ref, g_ref, o_ref, kbuf = refs
    n = q_ref.shape[1]
    lp = lam_ref[...]
    lam = (jnp.exp(jnp.sum(lp[0:1] * lp[1:2], axis=1, keepdims=True))
           - jnp.exp(jnp.sum(lp[2:3] * lp[3:4], axis=1, keepdims=True)) + lam_init)
    k = k_ref[0].astype(F32)
    if rope:
        k = _rope(k, cos_ref[...], sa_ref[...], sb_ref[...])
    kbuf[...] = k.astype(BF16)
    gain = g_ref[...] * (1.0 - lam_init)

    def body(i, carry):
        r0 = pl.multiple_of(i * tq, tq)
        q = q_ref[0, pl.ds(r0, tq), :].astype(F32)
        if rope:
            q = _rope(q, cos_ref[pl.ds(r0, tq), :], sa_ref[pl.ds(r0, tq), :], sb_ref[pl.ds(r0, tq), :])
        q = q * (DA_HD ** -0.5)
        lane = lax.broadcasted_iota(jnp.int32, q.shape, 1)
        es, invs = [], []
        for sub in range(2):
            qm = jnp.where((lane >= DA_HD) == bool(sub), q, 0.0).astype(BF16)
            sl = _dot_nt(qm, kbuf[...])
            m = jnp.max(sl, axis=-1, keepdims=True)
            if n_pre:
                sc = _dot_nt(qm, kc_ref[0])
                m = jnp.maximum(m, jnp.max(sc, axis=-1, keepdims=True))
                ec = jnp.exp(sc - m)
            el = jnp.exp(sl - m)
            tot = jnp.sum(el, axis=-1, keepdims=True)
            if n_pre:
                tot = tot + jnp.sum(ec, axis=-1, keepdims=True)
            es.append((ec if n_pre else None, el))
            invs.append(1.0 / tot)
        c0 = invs[0]
        c1 = lam * invs[1]
        wl = (es[0][1] * c0 - es[1][1] * c1).astype(BF16)
        o = _dot(wl, v_ref[0])
        if n_pre:
            wc = (es[0][0] * c0 - es[1][0] * c1).astype(BF16)
            o = o + _dot(wc, vc_ref[0])
        o = o * lax.rsqrt(jnp.mean(o * o, axis=-1, keepdims=True) + LN_EPS) * gain
        o_ref[0, pl.ds(r0, tq), :] = o.astype(BF16)
        return carry

    lax.fori_loop(0, n // tq, body, 0)


def diff_attention(p, p_pre, tables, lam_params, subln_g, lam_init, rope):
    bsz, n, width = p.shape
    qc = (width // 2) // LANES
    nh = DA_HEADS
    n_pre = 0 if p_pre is None else p_pre.shape[1]
    tq = min(n, 256)
    col = lambda off: pl.BlockSpec((1, n, LANES), lambda b, h, off=off: (b, 0, qc + off + h))
    in_specs = [col(0), col(nh), col(2 * nh)]
    args = [p, p, p]
    if n_pre:
        pre = lambda off: pl.BlockSpec((1, n_pre, LANES), lambda b, h, off=off: (b, 0, qc + off + h))
        in_specs += [pre(nh), pre(2 * nh)]
        args += [p_pre, p_pre]
    full = lambda shape: pl.BlockSpec(shape, lambda b, h: (0,) * len(shape))
    in_specs += [full((n, LANES))] * 3 + [full((4, LANES)), full((1, LANES))]
    args += list(tables) + [lam_params, subln_g[None]]
    return pl.pallas_call(
        functools.partial(_attn_kernel, n_pre=n_pre, rope=rope, lam_init=lam_init, tq=tq),
        grid=(bsz, nh),
        in_specs=in_specs,
        out_specs=pl.BlockSpec((1, n, LANES), lambda b, h: (b, 0, h)),
        out_shape=jax.ShapeDtypeStruct((bsz, n, nh * LANES), BF16),
        scratch_shapes=[pltpu.VMEM((n, LANES), BF16)],
        compiler_params=_cparams(("arbitrary", "arbitrary")),
        name="diff_attention",
    )(*args)


def _ab_out_kernel(a_ref, b_ref, w_ref, x_ref, g_ref, lg_ref, lb_ref, o_ref):
    ka = a_ref.shape[2]
    m = _dot(a_ref[0], w_ref[0:ka, :]) + _dot(b_ref[0], w_ref[ka:, :])
    o_ref[0] = _layer_norm(DN_ALPHA * x_ref[0] + g_ref[0] * m, lg_ref[...], lb_ref[...])


def _rg_out_kernel(gate_ref, h_ref, w_ref, x_ref, g_ref, lg_ref, lb_ref, o_ref):
    a = (gate_ref[0].astype(F32) * h_ref[0]).astype(BF16)
    m = _dot(a, w_ref[...])
    o_ref[0] = _layer_norm(DN_ALPHA * x_ref[0] + g_ref[0] * m, lg_ref[...], lb_ref[...])


def out_proj_norm(kern, a, b, w, x, g, ln_g, ln_b, name):
    bsz, n, d = x.shape
    tm = min(n, 512)
    row = lambda arr: pl.BlockSpec((1, tm, arr.shape[2]), lambda bb, i: (bb, i, 0))
    vec = pl.BlockSpec((1, d), lambda bb, i: (0, 0))
    return pl.pallas_call(
        kern,
        grid=(bsz, n // tm),
        in_specs=[row(a), row(b), pl.BlockSpec(w.shape, lambda bb, i: (0, 0)), row(x),
                  pl.BlockSpec((1, 1, d), lambda bb, i: (bb, 0, 0)), vec, vec],
        out_specs=row(x),
        out_shape=jax.ShapeDtypeStruct(x.shape, F32),
        compiler_params=_cparams(("arbitrary", "arbitrary")),
        name=name,
    )(a, b, w, x, g, ln_g[None], ln_b[None])


def _rg_in_kernel(x_ref, sc_ref, sh_ref, w_ref, gate_ref, rec_ref):
    d = x_ref.shape[2]
    h = (x_ref[0] * (1.0 + sc_ref[0]) + sh_ref[0]).astype(BF16)
    y = _dot(h, w_ref[:, 0:d])
    gelu = 0.5 * y * (1.0 + jnp.tanh(math.sqrt(2.0 / math.pi) * (y + 0.044715 * (y * y * y))))
    gate_ref[0] = gelu.astype(BF16)
    rec_ref[0] = _dot(h, w_ref[:, d:])


def rg_in(x, sc, sh, w):
    bsz, n, d = x.shape
    tm = min(n, 512)
    row = pl.BlockSpec((1, tm, d), lambda b, i: (b, i, 0))
    vec = pl.BlockSpec((1, 1, d), lambda b, i: (b, 0, 0))
    return pl.pallas_call(
        _rg_in_kernel,
        grid=(bsz, n // tm),
        in_specs=[row, vec, vec, pl.BlockSpec(w.shape, lambda b, i: (0, 0))],
        out_specs=[row, row],
        out_shape=[jax.ShapeDtypeStruct((bsz, n, d), BF16), jax.ShapeDtypeStruct((bsz, n, d), F32)],
        compiler_params=_cparams(("arbitrary", "arbitrary")),
        name="rg_in",
    )(x, sc, sh, w)


def _rg_scan_kernel(rc_ref, rl_ref, cw_ref, cb_ref, wa_ref, wx_ref, ba_ref, bx_ref, lam_ref,
                    hc_ref, hl_ref, uc_ref, ul_ref, a_ref, b_ref, *, chunk):
    d = rc_ref.shape[2]
    bw = d // RG_BLOCKS

    def conv(r):
        cw = cw_ref[...]
        return (cw[0:1] * _shift_rows(r, 2) + cw[1:2] * _shift_rows(r, 1) + cw[2:3] * r
                + cw[3:4] * _shift_rows(r, -1) + cb_ref[...])

    uc_ref[...] = conv(rc_ref[0])
    ul_ref[...] = conv(rl_ref[0])

    def coeffs(u_ref, r0, dr):
        u = u_ref[pl.ds(r0, chunk), :]
        ub = u.astype(BF16)
        lam = lam_ref[dr:dr + 1, :]
        log_sig = jnp.minimum(lam, 0.0) - jnp.log(1.0 + jnp.exp(-jnp.abs(lam)))
        for kb in range(RG_BLOCKS):
            sl = slice(kb * bw, (kb + 1) * bw)
            r = jax.nn.sigmoid(_dot(ub[:, sl], wa_ref[dr, kb]) + ba_ref[dr:dr + 1, sl])
            i = jax.nn.sigmoid(_dot(ub[:, sl], wx_ref[dr, kb]) + bx_ref[dr:dr + 1, sl])
            log_a = RG_C * r * log_sig[:, sl]
            a = jnp.exp(log_a)
            a_ref[:, sl] = a
            b_ref[:, sl] = jnp.sqrt(1.0 - a * a) * (i * u[:, sl])

    def run(u_ref, o_ref, h, dr):
        n = u_ref.shape[0]
        nchunk = n // chunk

        def chunk_body(ci, h):
            c = ci if dr == 0 else nchunk - 1 - ci
            r0 = pl.multiple_of(c * chunk, chunk)
            coeffs(u_ref, r0, dr)

            def row_body(ti, h):
                t = ti if dr == 0 else chunk - 1 - ti
                h = a_ref[pl.ds(t, 1), :] * h + b_ref[pl.ds(t, 1), :]
                if dr == 0:
                    o_ref[0, pl.ds(r0 + t, 1), :] = h
                else:
                    o_ref[0, pl.ds(r0 + t, 1), :] += h
                return h

            return lax.fori_loop(0, chunk, row_body, h)

        return lax.fori_loop(0, nchunk, chunk_body, h)

    for dr in range(2):
        h = jnp.zeros((1, d), F32)
        h = run(uc_ref, hc_ref, h, dr)
        run(ul_ref, hl_ref, h, dr)


def rg_scan(rec_c, rec_l, conv_w, conv_b, wa, wx, ba, bx, lam):
    bsz, nc, d = rec_c.shape
    nl = rec_l.shape[1]
    chunk = 256
    full = lambda arr: pl.BlockSpec(arr.shape, lambda b: (0,) * arr.ndim)
    seq = lambda n: pl.BlockSpec((1, n, d), lambda b: (b, 0, 0), pipeline_mode=pl.Buffered(1))
    cb = conv_b[None]
    return pl.pallas_call(
        functools.partial(_rg_scan_kernel, chunk=chunk),
        grid=(bsz,),
        in_specs=[seq(nc), seq(nl), full(conv_w), full(cb), full(wa), full(wx), full(ba), full(bx),
                  full(lam)],
        out_specs=[seq(nc), seq(nl)],
        out_shape=[jax.ShapeDtypeStruct((bsz, nc, d), F32), jax.ShapeDtypeStruct((bsz, nl, d), F32)],
        scratch_shapes=[pltpu.VMEM((nc, d), F32), pltpu.VMEM((nl, d), F32),
                        pltpu.VMEM((chunk, d), F32), pltpu.VMEM((chunk, d), F32)],
        compiler_params=_cparams(("arbitrary",)),
        name="rg_scan",
    )(rec_c, rec_l, conv_w, cb, wa, wx, ba, bx, lam)


def _route_kernel(x_ref, sc_ref, sh_ref, wh_ref, wl_ref, bias_ref, gates_ref, rank_ref, rank_t_ref):
    t = x_ref[0] * (1.0 + sc_ref[0]) + sh_ref[0]
    th = t.astype(BF16)
    tl = (t - th.astype(F32)).astype(BF16)
    logits = _dot_nt(wh_ref[...], th) + (_dot_nt(wh_ref[...], tl) + _dot_nt(wl_ref[...], th))
    scores = jax.nn.sigmoid(logits)
    sel = scores + bias_ref[...]
    ninf = -jnp.inf
    sub = lax.broadcasted_iota(jnp.int32, (EPG, sel.shape[1]), 0)
    gscore = []
    for g in range(N_GROUPS):
        slab = sel[g * EPG:(g + 1) * EPG]
        m1 = jnp.max(slab, axis=0, keepdims=True)
        first = jnp.min(jnp.where(slab == m1, sub, EPG), axis=0, keepdims=True)
        m2 = jnp.max(jnp.where(sub == first, ninf, slab), axis=0, keepdims=True)
        gscore.append(m1 + m2)
    slabs = []
    for g in range(N_GROUPS):
        beaten = jnp.zeros(gscore[g].shape, F32)
        for o in range(N_GROUPS):
            if o != g:
                wins = (gscore[o] >= gscore[g]) if o < g else (gscore[o] > gscore[g])
                beaten = beaten + jnp.where(wins, 1.0, 0.0)
        keep = jnp.broadcast_to(beaten < TOPK_GROUPS, (EPG, sel.shape[1]))
        slabs.append(jnp.where(keep, sel[g * EPG:(g + 1) * EPG], ninf))
    v = jnp.concatenate(slabs, axis=0)
    eidx = lax.broadcasted_iota(jnp.int32, v.shape, 0)
    w = jnp.zeros(v.shape, F32)
    chosen = jnp.zeros(v.shape, F32)
    for _ in range(TOP_K):
        m = jnp.max(v, axis=0, keepdims=True)
        first = jnp.min(jnp.where(v == m, eidx, N_EXPERTS), axis=0, keepdims=True)
        hit = eidx == first
        w = jnp.where(hit, scores, w)
        chosen = jnp.where(hit, 1.0, chosen)
        v = jnp.where(hit, ninf, v)
    gates = w / jnp.sum(w, axis=0, keepdims=True) * ROUTED_SCALE
    tm = v.shape[1]
    upper = jnp.where(lax.broadcasted_iota(jnp.int32, (tm, tm), 0) < lax.broadcasted_iota(jnp.int32, (tm, tm), 1),
                      1.0, 0.0).astype(BF16)
    rank = jnp.where(chosen > 0.0, _dot(chosen.astype(BF16), upper), -1.0)
    pad = LANES - N_EXPERTS
    gates_ref[...] = jnp.concatenate([gates, jnp.zeros((pad, tm), F32)], axis=0).T
    rank_p = jnp.concatenate([rank, jnp.full((pad, tm), -1.0, F32)], axis=0)
    rank_ref[...] = rank_p.T.astype(BF16)
    rank_t_ref[...] = rank_p.astype(BF16)


MOE_WIN = 256
SEG_ALIGN = 16
MOE_TILE = 512
WIN_ROWS = -(-(MOE_WIN * TOP_K + N_EXPERTS * (SEG_ALIGN - 1)) // 256) * 256
SEG_SIZES = tuple(SEG_ALIGN << k for k in reversed(range((MOE_WIN // SEG_ALIGN).bit_length())))


def route(x, sc, sh, rw_hi, rw_lo, bias):
    bsz, n, d = x.shape
    tm = MOE_WIN
    nt = n // tm
    full = lambda arr: pl.BlockSpec(arr.shape, lambda b, i: (0,) * arr.ndim)
    vec = pl.BlockSpec((1, 1, d), lambda b, i: (b, 0, 0))
    tok = pl.BlockSpec((tm, LANES), lambda b, i: (b * nt + i, 0))
    return pl.pallas_call(
        _route_kernel,
        grid=(bsz, nt),
        in_specs=[pl.BlockSpec((1, tm, d), lambda b, i: (b, i, 0)), vec, vec, full(rw_hi), full(rw_lo),
                  full(bias)],
        out_specs=[tok, tok, pl.BlockSpec((LANES, tm), lambda b, i: (0, b * nt + i))],
        out_shape=[jax.ShapeDtypeStruct((bsz * n, LANES), F32), jax.ShapeDtypeStruct((bsz * n, LANES), BF16),
                   jax.ShapeDtypeStruct((LANES, bsz * n), BF16)],
        compiler_params=_cparams(("arbitrary", "arbitrary")),
        name="route",
    )(x, sc, sh, rw_hi, rw_lo, bias)


def moe_plan(rank_tm, nw):
    cnt = (rank_tm[:, :N_EXPERTS] >= 0).reshape(nw, MOE_WIN, N_EXPERTS).sum(axis=1).astype(jnp.int32)
    pc = (cnt + SEG_ALIGN - 1) // SEG_ALIGN * SEG_ALIGN
    lo = jnp.cumsum(pc, axis=1) - pc
    used = pc.sum(axis=0)
    tot = (used + MOE_TILE - 1) // MOE_TILE * MOE_TILE
    ends = jnp.cumsum(tot)
    go = (ends - tot)[None, :] + jnp.cumsum(pc, axis=0) - pc
    n_tiles = (nw * MOE_WIN * TOP_K + nw * N_EXPERTS * (SEG_ALIGN - 1)) // MOE_TILE + N_EXPERTS
    tile_expert = jnp.sum(jnp.arange(n_tiles, dtype=jnp.int32)[:, None] >= (ends // MOE_TILE)[None, :], axis=1)
    col = lambda a: jnp.pad(a.astype(F32), ((0, 0), (0, LANES - N_EXPERTS)))
    return dict(
        lo=lo.reshape(-1), pc=pc.reshape(-1), go=go.reshape(-1),
        tail_pc=tot - used, tail_go=ends - tot + used,
        lo_row=col(lo)[:, None, :], pc_row=col(pc)[:, None, :],
        lo_col=col(lo)[:, :, None], pc_col=col(pc)[:, :, None],
        tile_expert=jnp.minimum(tile_expert, N_EXPERTS - 1).astype(jnp.int32),
        n_used=(ends[-1] // MOE_TILE).astype(jnp.int32).reshape(1), n_tiles=n_tiles)


def _segment_copies(lo_s, pc_s, go_s, w, local_ref, sorted_hbm, sem, to_sorted, wait):
    def body(e, carry):
        idx = w * N_EXPERTS + e
        lo = 0 if lo_s is None else lo_s[idx]
        pc = pc_s[idx]
        go = go_s[idx]
        off = jnp.int32(0)
        for size in SEG_SIZES:
            bit = pc & size

            @pl.when(bit != 0)
            def _(off=off, size=size):
                loc = local_ref.at[pl.ds(pl.multiple_of(lo + off, SEG_ALIGN), size), :]
                glo = sorted_hbm.at[pl.ds(pl.multiple_of(go + off, SEG_ALIGN), size), :]
                cp = pltpu.make_async_copy(loc, glo, sem) if to_sorted else pltpu.make_async_copy(glo, loc, sem)
                if wait:
                    cp.wait()
                else:
                    cp.start()

            off = off + bit
        return carry

    lax.fori_loop(0, N_EXPERTS, body, 0)


def _dispatch_kernel(lo_s, pc_s, go_s, tpc_s, tgo_s, x_ref, sc_ref, sh_ref, rank_t_ref, lo_ref, pc_ref,
                     sorted_hbm, rows_ref, sem):
    w = pl.program_id(0) * pl.num_programs(1) + pl.program_id(1)

    @pl.when(w == 0)
    def _():
        rows_ref[0:MOE_TILE, :] = jnp.zeros((MOE_TILE, rows_ref.shape[1]), BF16)
        _segment_copies(None, tpc_s, tgo_s, 0, rows_ref, sorted_hbm, sem.at[0], True, False)
        _segment_copies(None, tpc_s, tgo_s, 0, rows_ref, sorted_hbm, sem.at[0], True, True)

    t = (x_ref[0] * (1.0 + sc_ref[0]) + sh_ref[0]).astype(BF16)
    lo = lo_ref[0]
    hi = lo + pc_ref[0]
    chunk = 512
    for c in range(WIN_ROWS // chunk):
        r = (lax.broadcasted_iota(jnp.int32, (chunk, LANES), 0) + c * chunk).astype(F32)
        owner = (r >= lo) & (r < hi)
        seg_lo = jnp.sum(jnp.where(owner, lo, 0.0), axis=1, keepdims=True)
        rank_of = _dot(jnp.where(owner, 1.0, 0.0).astype(BF16), rank_t_ref[...])
        pick = jnp.where(rank_of == r[:, 0:1] - seg_lo, 1.0, 0.0).astype(BF16)
        rows_ref[c * chunk:(c + 1) * chunk, :] = _dot(pick, t).astype(BF16)
    _segment_copies(lo_s, pc_s, go_s, w, rows_ref, sorted_hbm, sem.at[0], True, False)
    _segment_copies(lo_s, pc_s, go_s, w, rows_ref, sorted_hbm, sem.at[0], True, True)


def moe_dispatch(x, sc, sh, rank_t, plan, n_rows):
    bsz, n, d = x.shape
    nwb = n // MOE_WIN
    vec = pl.BlockSpec((1, 1, d), lambda b, i, *_: (b, 0, 0))
    meta = pl.BlockSpec((1, 1, LANES), lambda b, i, *_: (b * nwb + i, 0, 0))
    return pl.pallas_call(
        _dispatch_kernel,
        grid_spec=pltpu.PrefetchScalarGridSpec(
            num_scalar_prefetch=5,
            grid=(bsz, nwb),
            in_specs=[pl.BlockSpec((1, MOE_WIN, d), lambda b, i, *_: (b, i, 0)), vec, vec,
                      pl.BlockSpec((LANES, MOE_WIN), lambda b, i, *_: (0, b * nwb + i)), meta, meta],
            out_specs=pl.BlockSpec(memory_space=pl.ANY),
            scratch_shapes=[pltpu.VMEM((WIN_ROWS, d), BF16), pltpu.SemaphoreType.DMA((1,))]),
        out_shape=jax.ShapeDtypeStruct((n_rows, d), BF16),
        compiler_params=_cparams(("arbitrary", "arbitrary")),
        name="moe_dispatch",
    )(plan["lo"], plan["pc"], plan["go"], plan["tail_pc"], plan["tail_go"], x, sc, sh, rank_t,
      plan["lo_row"], plan["pc_row"])


def _expert_kernel(te_s, nu_s, x_ref, wg_ref, wu_ref, wd_ref, o_ref):
    @pl.when(pl.program_id(0) < nu_s[0])
    def _():
        x = x_ref[...]
        h = _silu(_dot(x, wg_ref[0, 0].astype(BF16))) * _dot(x, wu_ref[0, 0].astype(BF16))
        o_ref[...] = _dot(h.astype(BF16), wd_ref[0, 0].astype(BF16)).astype(BF16)


def moe_experts(xs, plan, layer, wg, wu, wd):
    n_rows, d = xs.shape
    de = wg.shape[3]
    tile = lambda i, te, nu: jnp.minimum(i, nu[0] - 1)
    row = pl.BlockSpec((MOE_TILE, d), lambda i, te, nu: (tile(i, te, nu), 0))
    return pl.pallas_call(
        _expert_kernel,
        grid_spec=pltpu.PrefetchScalarGridSpec(
            num_scalar_prefetch=2,
            grid=(plan["n_tiles"],),
            in_specs=[row,
                      pl.BlockSpec((1, 1, d, de), lambda i, te, nu: (layer, te[tile(i, te, nu)], 0, 0)),
                      pl.BlockSpec((1, 1, d, de), lambda i, te, nu: (layer, te[tile(i, te, nu)], 0, 0)),
                      pl.BlockSpec((1, 1, de, d), lambda i, te, nu: (layer, te[tile(i, te, nu)], 0, 0))],
            out_specs=row),
        out_shape=jax.ShapeDtypeStruct((n_rows, d), BF16),
        compiler_params=_cparams(("arbitrary",)),
        name="moe_experts",
    )(plan["tile_expert"], plan["n_used"], xs, wg, wu, wd)


def _combine_kernel(lo_s, pc_s, go_s, x_ref, sc_ref, sh_ref, gf_ref, gates_ref, rank_ref, lo_ref, pc_ref,
                    sorted_hbm, sg_ref, su_ref, sd_ref, lg_ref, lb_ref, o_ref, rows_ref, sem):
    first = (pl.program_id(0) == 0) & (pl.program_id(1) == 0)

    @pl.when(first)
    def _():
        rows_ref[...] = jnp.zeros_like(rows_ref)

    w = pl.program_id(0) * pl.num_programs(1) + pl.program_id(1)
    _segment_copies(lo_s, pc_s, go_s, w, rows_ref, sorted_hbm, sem.at[0], False, False)
    x = x_ref[0]
    t = (x * (1.0 + sc_ref[0]) + sh_ref[0]).astype(BF16)
    shared = _dot((_silu(_dot(t, sg_ref[0].astype(BF16))) * _dot(t, su_ref[0].astype(BF16))).astype(BF16),
                  sd_ref[0].astype(BF16))
    r = lax.broadcasted_iota(jnp.int32, (LANES, WIN_ROWS), 1).astype(F32)
    lo = lo_ref[0]
    owner = (r >= lo) & (r < lo + pc_ref[0])
    seg_lo = jnp.sum(jnp.where(owner, lo, 0.0), axis=0, keepdims=True)
    owner_b = jnp.where(owner, 1.0, 0.0).astype(BF16)
    rank_of = _dot(rank_ref[...], owner_b)
    gate_of = _dot(gates_ref[...].astype(BF16), owner_b)
    weights = jnp.where(rank_of == r[0:1, :] - seg_lo, gate_of, 0.0).astype(BF16)
    _segment_copies(lo_s, pc_s, go_s, w, rows_ref, sorted_hbm, sem.at[0], False, True)
    routed = _dot(weights, rows_ref[...])
    o_ref[0] = _layer_norm(DN_ALPHA * x + gf_ref[0] * (shared + routed), lg_ref[...], lb_ref[...])


def moe_combine(x, sc, sh, gf, gates, rank_tm, plan, ys, layer, sg, su, sd, ln_g, ln_b):
    bsz, n, d = x.shape
    nwb = n // MOE_WIN
    row = pl.BlockSpec((1, MOE_WIN, d), lambda b, i, *_: (b, i, 0))
    vec = pl.BlockSpec((1, 1, d), lambda b, i, *_: (b, 0, 0))
    tok = pl.BlockSpec((MOE_WIN, LANES), lambda b, i, *_: (b * nwb + i, 0))
    meta = pl.BlockSpec((1, LANES, 1), lambda b, i, *_: (b * nwb + i, 0, 0))
    lnv = pl.BlockSpec((1, d), lambda b, i, *_: (0, 0))
    return pl.pallas_call(
        _combine_kernel,
        grid_spec=pltpu.PrefetchScalarGridSpec(
            num_scalar_prefetch=3,
            grid=(bsz, nwb),
            in_specs=[row, vec, vec, vec, tok, tok, meta, meta,
                      pl.BlockSpec(memory_space=pl.ANY),
                      pl.BlockSpec((1, d, sg.shape[2]), lambda b, i, *_: (layer, 0, 0)),
                      pl.BlockSpec((1, d, su.shape[2]), lambda b, i, *_: (layer, 0, 0)),
                      pl.BlockSpec((1, sd.shape[1], d), lambda b, i, *_: (layer, 0, 0)),
                      lnv, lnv],
            out_specs=row,
            scratch_shapes=[pltpu.VMEM((WIN_ROWS, d), BF16), pltpu.SemaphoreType.DMA((1,))]),
        out_shape=jax.ShapeDtypeStruct(x.shape, F32),
        compiler_params=_cparams(("arbitrary", "arbitrary")),
        name="moe_combine",
    )(plan["lo"], plan["pc"], plan["go"], x, sc, sh, gf, gates, rank_tm, plan["lo_col"], plan["pc_col"], ys,
      sg, su, sd, ln_g[None], ln_b[None])


def kernel(x, c, ctx, c_ctx, w_mod, b_mod, ln1_g, ln1_b, ln2_g, ln2_b, ab_w_in, ab_w_out, hy_short_w, hy_short_b, hy_fw1, hy_fb1, hy_fw2, hy_fb2, hy_fw3, hy_fb3, hy_fw4, hy_freq, hy_decay, hy_skip, da_lq1, da_lk1, da_lq2, da_lk2, da_subln_g, rg_w_in, rg_w_out, rg_conv_w, rg_conv_b, rg_wa, rg_ba, rg_wx, rg_bx, rg_lambda, router_w, router_bias, ex_w_gate, ex_w_up, ex_w_down, sh_w_gate, sh_w_up, sh_w_down):
    return forward(dict(locals()))


def split_mod(mod_l, bsz, d):
    ml = [mod_l[:bsz, k * d:(k + 1) * d][:, None, :] for k in range(N_MOD)]
    mc = [jnp.broadcast_to(mod_l[bsz, k * d:(k + 1) * d][None, None, :], (bsz, 1, d)) for k in range(N_MOD)]
    return ml, mc


def mixer_even(p, l, xs, cs, ml, mc, need_ctx):
    e = l // 2
    n_lat, n_ctx = xs.shape[1], cs.shape[1]
    lam_init = 0.8 - 0.6 * math.exp(-0.3 * l)
    w_in = p["ab_w_in"][e].astype(BF16)
    w_out = p["ab_w_out"][e].astype(BF16)
    p_lat = ab_in(xs, ml[1], ml[0], w_in)
    p_ctx = ab_in(cs, mc[1], mc[0], w_in)
    filt_p = tuple(p[k][e] for k in ("hy_fw1", "hy_fb1", "hy_fw2", "hy_fb2", "hy_fw3", "hy_fb3", "hy_fw4",
                                     "hy_freq", "hy_decay"))
    lam_params = jnp.pad(jnp.stack([p["da_lq1"][e], p["da_lk1"][e], p["da_lq2"][e], p["da_lk2"][e]]),
                         ((0, 0), (0, LANES - DA_HD)))
    rope_lat = rope_tables(n_lat)
    rope_ctx = tuple(t[:n_ctx] for t in rope_lat)

    def hyena(proj, n):
        dft = dft_tables(n)
        hs, hd = hyena_filter_parts(n, *filt_p)
        tabs = filter_spectrum(hs, hd, dft[0], dft[1])
        return hyena_mixer(proj, p["hy_short_w"][e], p["hy_short_b"][e], p["hy_skip"][e], tabs, *dft)

    g1, b1 = p["ln1_g"][l], p["ln1_b"][l]
    sub_g = p["da_subln_g"][e]
    y_lat = hyena(p_lat, n_lat)
    o_lat = diff_attention(p_lat, p_ctx, rope_lat, lam_params, sub_g, lam_init, True)
    xs_new = out_proj_norm(_ab_out_kernel, y_lat, o_lat, w_out, xs, ml[2], g1, b1, "ab_out")
    if need_ctx:
        y_ctx = hyena(p_ctx, n_ctx)
        o_ctx = diff_attention(p_ctx, None, rope_ctx, lam_params, sub_g, lam_init, False)
        cs = out_proj_norm(_ab_out_kernel, y_ctx, o_ctx, w_out, cs, mc[2], g1, b1, "ab_out")
    return xs_new, cs


def mixer_odd(p, l, xs, cs, ml, mc, need_ctx):
    o = l // 2
    w_in = p["rg_w_in"][o].astype(BF16)
    w_out = p["rg_w_out"][o].astype(BF16)
    gate_l, rec_l = rg_in(xs, ml[1], ml[0], w_in)
    gate_c, rec_c = rg_in(cs, mc[1], mc[0], w_in)
    hc, hl = rg_scan(rec_c, rec_l, p["rg_conv_w"][o], p["rg_conv_b"][o], p["rg_wa"][o].astype(BF16),
                     p["rg_wx"][o].astype(BF16), p["rg_ba"][o], p["rg_bx"][o], p["rg_lambda"][o])
    g1, b1 = p["ln1_g"][l], p["ln1_b"][l]
    xs = out_proj_norm(_rg_out_kernel, gate_l, hl, w_out, xs, ml[2], g1, b1, "rg_out")
    if need_ctx:
        cs = out_proj_norm(_rg_out_kernel, gate_c, hc, w_out, cs, mc[2], g1, b1, "rg_out")
    return xs, cs


def moe_block(p, l, h, m):
    rw_t = p["router_w"][l].T
    rw_hi = rw_t.astype(BF16)
    rw_lo = (rw_t - rw_hi.astype(F32)).astype(BF16)
    gates, rank_tm, rank_t = route(h, m[4], m[3], rw_hi, rw_lo, p["router_bias"][l][:, None])
    plan = moe_plan(rank_tm, gates.shape[0] // MOE_WIN)
    xs = moe_dispatch(h, m[4], m[3], rank_t, plan, plan["n_tiles"] * MOE_TILE)
    ys = moe_experts(xs, plan, l, p["ex_w_gate"], p["ex_w_up"], p["ex_w_down"])
    return moe_combine(h, m[4], m[3], m[5], gates, rank_tm, plan, ys, l, p["sh_w_gate"], p["sh_w_up"],
                       p["sh_w_down"], p["ln2_g"][l], p["ln2_b"][l])


def run_layer(p, l, xs, cs, mod_l):
    bsz, _, d = xs.shape
    need_ctx = l < DEPTH - 1
    ml, mc = split_mod(mod_l, bsz, d)
    mixer = mixer_even if l % 2 == 0 else mixer_odd
    xs, cs = mixer(p, l, xs, cs, ml, mc, need_ctx)
    xs = moe_block(p, l, xs, ml)
    if need_ctx:
        cs = moe_block(p, l, cs, mc)
    return xs, cs


def conditioning_rows(c, c_ctx):
    bsz, d = c.shape
    rows = 8 * ((bsz + 1 + 7) // 8)
    return jnp.zeros((rows, d), F32).at[:bsz].set(c).at[bsz].set(c_ctx)


def forward(p):
    mod = modulation(conditioning_rows(p["c"], p["c_ctx"]), p["w_mod"], p["b_mod"])
    xs, cs = p["x"], p["ctx"]
    for l in range(DEPTH):
        xs, cs = run_layer(p, l, xs, cs, mod[l])
    return xs
```

```python
import functools
import math

import jax
import jax.numpy as jnp
from jax import lax
from jax.experimental import pallas as pl
from jax.experimental.pallas import tpu as pltpu

F32 = jnp.float32
BF16 = jnp.bfloat16

DEPTH = 4
GRID_W = 64
N_MOD = 6
LN_EPS = 1e-5
DN_ALPHA = (2.0 * DEPTH) ** 0.25

HY_ORDER = 2
HY_EMB = 33
HY_BANDS = (HY_EMB - 1) // 2
DA_HEADS = 4
DA_HD = 64
DA_VD = 2 * DA_HD
ROPE_BASE = 10000.0
ROPE_NF = DA_HD // 4
RG_BLOCKS = 8
RG_C = 8.0
N_EXPERTS = 64
N_GROUPS = 8
EPG = N_EXPERTS // N_GROUPS
TOPK_GROUPS = 4
TOP_K = 8
ROUTED_SCALE = 2.5

LANES = 128
VMEM_LIMIT = 56 * 1024 * 1024


def _cparams(sem):
    return pltpu.CompilerParams(dimension_semantics=sem, vmem_limit_bytes=VMEM_LIMIT)


def _dot(a, b):
    return jnp.dot(a, b, preferred_element_type=F32)


def _dot_nt(a, b):
    return lax.dot_general(a, b, (((1,), (1,)), ((), ())), preferred_element_type=F32)


def _silu(x):
    return x * jax.nn.sigmoid(x)


def _sigmoid(x):
    return 0.5 * jnp.tanh(0.5 * x) + 0.5


def _layer_norm(r, g, b):
    mu = jnp.mean(r, axis=-1, keepdims=True)
    d = r - mu
    var = jnp.mean(d * d, axis=-1, keepdims=True)
    return d * lax.rsqrt(var + LN_EPS) * g + b


def _mod_kernel(cc_ref, w_ref, b_ref, o_ref):
    s = _silu(cc_ref[...])
    o_ref[0] = _dot(s.astype(BF16), w_ref[0].astype(BF16)) + b_ref[0]


def modulation(cc, w_mod, b_mod):
    r, d = cc.shape
    n = w_mod.shape[-1]
    tn = 1536
    return pl.pallas_call(
        _mod_kernel,
        grid=(DEPTH, n // tn),
        in_specs=[pl.BlockSpec((r, d), lambda l, j: (0, 0)),
                  pl.BlockSpec((1, d, tn), lambda l, j: (l, 0, j)),
                  pl.BlockSpec((1, 1, tn), lambda l, j: (l, 0, j))],
        out_specs=pl.BlockSpec((1, r, tn), lambda l, j: (l, 0, j)),
        out_shape=jax.ShapeDtypeStruct((DEPTH, r, n), F32),
        compiler_params=_cparams(("arbitrary", "arbitrary")),
        name="modulation",
    )(cc, w_mod, b_mod.reshape(DEPTH, 1, n))


def _ab_in_kernel(x_ref, sc_ref, sh_ref, w_ref, o_ref, *, tn):
    h = (x_ref[0] * (1.0 + sc_ref[0]) + sh_ref[0]).astype(BF16)
    for j in range(w_ref.shape[1] // tn):
        o_ref[0, :, j * tn:(j + 1) * tn] = _dot(h, w_ref[:, j * tn:(j + 1) * tn]).astype(BF16)


def ab_in(x, sc, sh, w):
    bsz, n, d = x.shape
    nout = w.shape[1]
    tm = min(n, 512)
    return pl.pallas_call(
        functools.partial(_ab_in_kernel, tn=1024),
        grid=(bsz, n // tm),
        in_specs=[pl.BlockSpec((1, tm, d), lambda b, i: (b, i, 0)),
                  pl.BlockSpec((1, 1, d), lambda b, i: (b, 0, 0)),
                  pl.BlockSpec((1, 1, d), lambda b, i: (b, 0, 0)),
                  pl.BlockSpec((d, nout), lambda b, i: (0, 0))],
        out_specs=pl.BlockSpec((1, tm, nout), lambda b, i: (b, i, 0)),
        out_shape=jax.ShapeDtypeStruct((bsz, n, nout), BF16),
        compiler_params=_cparams(("arbitrary", "arbitrary")),
        name="ab_in",
    )(x, sc, sh, w)


def _filter_kernel(z_ref, w1_ref, b1_ref, w2_ref, b2_ref, w3_ref, b3_ref, w4_ref, fr_ref, dec_ref,
                   hs_ref, hd_ref):
    hp = lax.Precision.HIGHEST
    z = z_ref[...]
    fr = fr_ref[...]
    h = jnp.sin(fr * (jnp.dot(z, w1_ref[...], precision=hp, preferred_element_type=F32) + b1_ref[...]))
    h = jnp.sin(fr * (jnp.dot(h, w2_ref[...], precision=hp, preferred_element_type=F32) + b2_ref[...]))
    h = jnp.sin(fr * (jnp.dot(h, w3_ref[...], precision=hp, preferred_element_type=F32) + b3_ref[...]))
    t = z[:, 0:1]
    f = jnp.dot(h, w4_ref[...], precision=hp, preferred_element_type=F32) * jnp.exp(-t * jnp.abs(dec_ref[...]))
    w = f.shape[1] // (2 * HY_ORDER)
    for o in range(HY_ORDER):
        hf = f[:, (2 * o) * w:(2 * o + 1) * w]
        hb = f[:, (2 * o + 1) * w:(2 * o + 2) * w]
        hs_ref[o] = (hf + hb).astype(BF16)
        hd_ref[o] = (hb - hf).astype(BF16)


def hyena_filter_parts(n, w1, b1, w2, b2, w3, b3, w4, freq, decay):
    t = jnp.linspace(0.0, 1.0, n, dtype=F32)[:, None]
    f = jnp.linspace(1e-4, HY_BANDS - 1, HY_BANDS, dtype=F32)
    ang = (2.0 * math.pi / n) * jnp.arange(n, dtype=F32)[:, None] * f[None, :]
    z = jnp.concatenate([t, jnp.cos(ang), -jnp.sin(ang)], axis=-1)
    z = jnp.pad(z, ((0, 0), (0, LANES - HY_EMB)))
    w1p = jnp.pad(w1, ((0, LANES - HY_EMB), (0, 0)))
    hid = w1.shape[1]
    nf = w4.shape[1]
    w = nf // (2 * HY_ORDER)
    tm = min(n, 512)
    full = lambda shape: pl.BlockSpec(shape, lambda i: (0,) * len(shape))
    return pl.pallas_call(
        _filter_kernel,
        grid=(n // tm,),
        in_specs=[pl.BlockSpec((tm, LANES), lambda i: (i, 0)),
                  full((LANES, hid)), full((1, hid)), full((hid, hid)), full((1, hid)),
                  full((hid, hid)), full((1, hid)), full((hid, nf)), full((1, hid)), full((1, nf))],
        out_specs=[pl.BlockSpec((HY_ORDER, tm, w), lambda i: (0, i, 0)),
                   pl.BlockSpec((HY_ORDER, tm, w), lambda i: (0, i, 0))],
        out_shape=[jax.ShapeDtypeStruct((HY_ORDER, n, w), BF16)] * 2,
        compiler_params=_cparams(("arbitrary",)),
        name="hyena_filter",
    )(z, w1p, b1[None], w2, b2[None], w3, b3[None], w4, freq[None], decay[None])


def dft_tables(n):
    k = (jnp.arange(n, dtype=jnp.int32)[:, None] * jnp.arange(n, dtype=jnp.int32)[None, :]) % (2 * n)
    ang = k.astype(F32) * (math.pi / n)
    c = jnp.cos(ang)
    s = jnp.sin(ang)
    alt = jnp.where(jnp.arange(n) % 2 == 0, 1.0, -1.0).astype(F32)
    s_fwd = s.at[0, :].set(alt)
    s_inv = s.at[:, 0].set(alt)
    return c.astype(BF16), s_fwd.astype(BF16), s_inv.astype(BF16)


def _spectrum_kernel(hs_ref, hd_ref, c_ref, s_ref, o_ref, *, n):
    f0 = pl.program_id(1) * c_ref.shape[0]
    hs = hs_ref[0]
    hr = _dot(c_ref[...], hs)
    hq = _dot(s_ref[...], hs)
    hi = _dot(s_ref[...], hd_ref[0])
    row = lax.broadcasted_iota(jnp.int32, hr.shape, 0) + f0
    dc = row == 0
    inv = 1.0 / (2 * n)
    o_ref[0, 0] = jnp.where(dc, hr * inv, hr * (2.0 * inv))
    o_ref[0, 1] = jnp.where(dc, 0.0, hi * (2.0 * inv))
    o_ref[0, 2] = jnp.where(dc, hq * inv, hr * (2.0 * inv))


def filter_spectrum(hs, hd, c, s_fwd):
    _, n, w = hs.shape
    tf = min(n, 256)
    return pl.pallas_call(
        functools.partial(_spectrum_kernel, n=n),
        grid=(HY_ORDER, n // tf),
        in_specs=[pl.BlockSpec((1, n, w), lambda o, f: (o, 0, 0)),
                  pl.BlockSpec((1, n, w), lambda o, f: (o, 0, 0)),
                  pl.BlockSpec((tf, n), lambda o, f: (f, 0)),
                  pl.BlockSpec((tf, n), lambda o, f: (f, 0))],
        out_specs=pl.BlockSpec((1, 3, tf, w), lambda o, f: (o, 0, f, 0)),
        out_shape=jax.ShapeDtypeStruct((HY_ORDER, 3, n, w), F32),
        compiler_params=_cparams(("arbitrary", "arbitrary")),
        name="filter_spectrum",
    )(hs, hd, c, s_fwd)


def _shift_rows(u, k):
    n = u.shape[0]
    row = lax.broadcasted_iota(jnp.int32, u.shape, 0)
    r = pltpu.roll(u, k % n, 0)
    if k > 0:
        return jnp.where(row < k, 0.0, r)
    return jnp.where(row >= n + k, 0.0, r)


def _hyena_kernel(p_ref, sw_ref, sb_ref, skip_ref, c_ref, s_ref, ct_ref, st_ref, tab_ref, o_ref,
                  sig_ref, z_ref, acc_ref, *, w):
    ph = pl.program_id(1)
    f = pl.program_id(2)
    last = pl.num_programs(2) - 1

    def short_conv(lo):
        u = p_ref[0, :, lo:lo + w].astype(F32)
        cw = sw_ref[:, lo:lo + w]
        return (cw[0:1] * _shift_rows(u, 1) + cw[1:2] * u + cw[2:3] * _shift_rows(u, -1)
                + sb_ref[:, lo:lo + w])

    @pl.when((ph == 0) & (f == 0))
    def _():
        sig_ref[...] = short_conv(0).astype(BF16)

    @pl.when(f == 0)
    def _():
        acc_ref[...] = jnp.zeros_like(acc_ref)

    sig = sig_ref[...]
    p = _dot(c_ref[...], sig)
    q = _dot(s_ref[...], sig)
    hr = tab_ref[0, 0]
    hi = tab_ref[0, 1]
    hq = tab_ref[0, 2]
    zr = (p * hr + q * hi).astype(BF16)
    zq = (q * hq - p * hi).astype(BF16)
    acc_ref[...] += _dot(ct_ref[...], zr) + _dot(st_ref[...], zq)

    @pl.when((f == last) & (ph == 0))
    def _():
        z = short_conv(w) * (acc_ref[...] + skip_ref[0:1] * short_conv(0))
        z_ref[...] = z
        sig_ref[...] = z.astype(BF16)

    @pl.when((f == last) & (ph == 1))
    def _():
        o_ref[0] = (short_conv(2 * w) * (acc_ref[...] + skip_ref[1:2] * z_ref[...])).astype(BF16)


def hyena_mixer(p, short_w, short_b, skip, tabs, c, s_fwd, s_inv):
    bsz, n, _ = p.shape
    w = skip.shape[1]
    tf = min(n, 256)
    return pl.pallas_call(
        functools.partial(_hyena_kernel, w=w),
        grid=(bsz, HY_ORDER, n // tf),
        in_specs=[pl.BlockSpec((1, n, 3 * w), lambda b, o, f: (b, 0, 0)),
                  pl.BlockSpec((3, 3 * w), lambda b, o, f: (0, 0)),
                  pl.BlockSpec((1, 3 * w), lambda b, o, f: (0, 0)),
                  pl.BlockSpec((HY_ORDER, w), lambda b, o, f: (0, 0)),
                  pl.BlockSpec((tf, n), lambda b, o, f: (f, 0)),
                  pl.BlockSpec((tf, n), lambda b, o, f: (f, 0)),
                  pl.BlockSpec((n, tf), lambda b, o, f: (0, f)),
                  pl.BlockSpec((n, tf), lambda b, o, f: (0, f)),
                  pl.BlockSpec((1, 3, tf, w), lambda b, o, f: (o, 0, f, 0))],
        out_specs=pl.BlockSpec((1, n, w), lambda b, o, f: (b, 0, 0)),
        out_shape=jax.ShapeDtypeStruct((bsz, n, w), BF16),
        scratch_shapes=[pltpu.VMEM((n, w), BF16), pltpu.VMEM((n, w), F32), pltpu.VMEM((n, w), F32)],
        compiler_params=_cparams(("arbitrary", "arbitrary", "arbitrary")),
        name="hyena_mixer",
    )(p, short_w, short_b[None], skip, c, s_fwd, c, s_inv, tabs)


def rope_tables(n):
    pos = jnp.arange(n)
    rc = jnp.stack([(pos // GRID_W).astype(F32), (pos % GRID_W).astype(F32)], axis=1)
    inv = ROPE_BASE ** (-jnp.arange(ROPE_NF, dtype=F32) / ROPE_NF)
    ang = rc[:, :, None] * inv
    cos = jnp.cos(ang)
    sin = jnp.sin(ang)
    zero = jnp.zeros_like(sin)
    tile = lambda a, b: jnp.tile(jnp.stack([a, b], axis=2).reshape(n, 4 * ROPE_NF), (1, 2))
    return tile(cos, cos), tile(-sin, zero), tile(zero, sin)


def _rope(x, cos, sin_a, sin_b):
    return (x * cos + pltpu.roll(x, LANES - ROPE_NF, 1) * sin_a + pltpu.roll(x, ROPE_NF, 1) * sin_b)


def _attn_kernel(*refs, n_pre, rope, lam_init, tq):
    if n_pre:
        q_ref, k_ref, v_ref, kc_ref, vc_ref, cos_ref, sa_ref, sb_ref, lam_ref, g_ref, o_ref, kbuf = refs
    else:
        q_ref, k_ref, v_ref, cos_ref, sa_ref, sb_ref, lam_ref, g_ref, o_ref, kbuf = refs
    n = q_ref.shape[1]
    lp = lam_ref[...]
    lam = (jnp.exp(jnp.sum(lp[0:1] * lp[1:2], axis=1, keepdims=True))
           - jnp.exp(jnp.sum(lp[2:3] * lp[3:4], axis=1, keepdims=True)) + lam_init)
    k = k_ref[0].astype(F32)
    if rope:
        k = _rope(k, cos_ref[...], sa_ref[...], sb_ref[...])
    kbuf[...] = k.astype(BF16)
    gain = g_ref[...] * (1.0 - lam_init)

    def body(i, carry):
        r0 = pl.multiple_of(i * tq, tq)
        q = q_ref[0, pl.ds(r0, tq), :].astype(F32)
        if rope:
            q = _rope(q, cos_ref[pl.ds(r0, tq), :], sa_ref[pl.ds(r0, tq), :], sb_ref[pl.ds(r0, tq), :])
        q = q * (DA_HD ** -0.5)
        lane = lax.broadcasted_iota(jnp.int32, q.shape, 1)
        es, invs = [], []
        for sub in range(2):
            qm = jnp.where((lane >= DA_HD) == bool(sub), q, 0.0).astype(BF16)
            sl = _dot_nt(qm, kbuf[...])
            m = jnp.max(sl, axis=-1, keepdims=True)
            if n_pre:
                sc = _dot_nt(qm, kc_ref[0])
                m = jnp.maximum(m, jnp.max(sc, axis=-1, keepdims=True))
                ec = jnp.exp(sc - m)
            el = jnp.exp(sl - m)
            tot = jnp.sum(el, axis=-1, keepdims=True)
            if n_pre:
                tot = tot + jnp.sum(ec, axis=-1, keepdims=True)
            es.append((ec if n_pre else None, el))
            invs.append(1.0 / tot)
        c0 = invs[0]
        c1 = lam * invs[1]
        wl = (es[0][1] * c0 - es[1][1] * c1).astype(BF16)
        o = _dot(wl, v_ref[0])
        if n_pre:
            wc = (es[0][0] * c0 - es[1][0] * c1).astype(BF16)
            o = o + _dot(wc, vc_ref[0])
        o = o * lax.rsqrt(jnp.mean(o * o, axis=-1, keepdims=True) + LN_EPS) * gain
        o_ref[0, pl.ds(r0, tq), :] = o.astype(BF16)
        return carry

    lax.fori_loop(0, n // tq, body, 0)


def diff_attention(p, p_pre, tables, lam_params, subln_g, lam_init, rope):
    bsz, n, width = p.shape
    qc = (width // 2) // LANES
    nh = DA_HEADS
    n_pre = 0 if p_pre is None else p_pre.shape[1]
    tq = min(n, 256)
    col = lambda off: pl.BlockSpec((1, n, LANES), lambda b, h, off=off: (b, 0, qc + off + h))
    in_specs = [col(0), col(nh), col(2 * nh)]
    args = [p, p, p]
    if n_pre:
        pre = lambda off: pl.BlockSpec((1, n_pre, LANES), lambda b, h, off=off: (b, 0, qc + off + h))
        in_specs += [pre(nh), pre(2 * nh)]
        args += [p_pre, p_pre]
    full = lambda shape: pl.BlockSpec(shape, lambda b, h: (0,) * len(shape))
    in_specs += [full((n, LANES))] * 3 + [full((4, LANES)), full((1, LANES))]
    args += list(tables) + [lam_params, subln_g[None]]
    return pl.pallas_call(
        functools.partial(_attn_kernel, n_pre=n_pre, rope=rope, lam_init=lam_init, tq=tq),
        grid=(bsz, nh),
        in_specs=in_specs,
        out_specs=pl.BlockSpec((1, n, LANES), lambda b, h: (b, 0, h)),
        out_shape=jax.ShapeDtypeStruct((bsz, n, nh * LANES), BF16),
        scratch_shapes=[pltpu.VMEM((n, LANES), BF16)],
        compiler_params=_cparams(("arbitrary", "arbitrary")),
        name="diff_attention",
    )(*args)


def _ab_out_kernel(a_ref, b_ref, w_ref, x_ref, g_ref, lg_ref, lb_ref, o_ref):
    ka = a_ref.shape[2]
    m = _dot(a_ref[0], w_ref[0:ka, :]) + _dot(b_ref[0], w_ref[ka:, :])
    o_ref[0] = _layer_norm(DN_ALPHA * x_ref[0] + g_ref[0] * m, lg_ref[...], lb_ref[...])


def out_proj_norm(a, b, w, x, g, ln_g, ln_b):
    bsz, n, d = x.shape
    tm = min(n, 512)
    row = lambda arr: pl.BlockSpec((1, tm, arr.shape[2]), lambda bb, i: (bb, i, 0))
    vec = pl.BlockSpec((1, d), lambda bb, i: (0, 0))
    return pl.pallas_call(
        _ab_out_kernel,
        grid=(bsz, n // tm),
        in_specs=[row(a), row(b), pl.BlockSpec(w.shape, lambda bb, i: (0, 0)), row(x),
                  pl.BlockSpec((1, 1, d), lambda bb, i: (bb, 0, 0)), vec, vec],
        out_specs=row(x),
        out_shape=jax.ShapeDtypeStruct(x.shape, F32),
        compiler_params=_cparams(("arbitrary", "arbitrary")),
        name="ab_out",
    )(a, b, w, x, g, ln_g[None], ln_b[None])


def _rg_out_kernel(gate_ref, hf_ref, hb_ref, w_ref, x_ref, g_ref, lg_ref, lb_ref, o_ref):
    h = hf_ref[0].astype(F32) + hb_ref[0].astype(F32)
    m = _dot((gate_ref[0].astype(F32) * h).astype(BF16), w_ref[...])
    o_ref[0] = _layer_norm(DN_ALPHA * x_ref[0] + g_ref[0] * m, lg_ref[...], lb_ref[...])


def rg_out(gate, hf, hb, row0, w, x, g, ln_g, ln_b):
    bsz, n, d = x.shape
    tm = RG_TILE
    off = row0 // tm
    seq = pl.BlockSpec((1, tm, d), lambda bb, i: (bb, i + off, 0))
    row = pl.BlockSpec((1, tm, d), lambda bb, i: (bb, i, 0))
    vec = pl.BlockSpec((1, d), lambda bb, i: (0, 0))
    return pl.pallas_call(
        _rg_out_kernel,
        grid=(bsz, n // tm),
        in_specs=[seq, seq, seq, pl.BlockSpec(w.shape, lambda bb, i: (0, 0)), row,
                  pl.BlockSpec((1, 1, d), lambda bb, i: (bb, 0, 0)), vec, vec],
        out_specs=row,
        out_shape=jax.ShapeDtypeStruct(x.shape, F32),
        compiler_params=_cparams(("arbitrary", "arbitrary")),
        name="rg_out",
    )(gate, hf, hb, w, x, g, ln_g[None], ln_b[None])


RG_TILE = 256
RG_STEPS = 32
RG_HALO = 8


def _rg_in_kernel(xc_ref, xl_ref, scc_ref, shc_ref, scl_ref, shl_ref, w_ref, gate_ref, rec_ref, *, nct):
    d = xc_ref.shape[2]
    is_ctx = pl.program_id(1) < nct
    x = jnp.where(is_ctx, xc_ref[0], xl_ref[0])
    sc = jnp.where(is_ctx, scc_ref[0], scl_ref[0])
    sh = jnp.where(is_ctx, shc_ref[0], shl_ref[0])
    h = (x * (1.0 + sc) + sh).astype(BF16)
    y = _dot(h, w_ref[:, 0:d])
    gelu = 0.5 * y * (1.0 + jnp.tanh(math.sqrt(2.0 / math.pi) * (y + 0.044715 * (y * y * y))))
    gate_ref[0] = gelu.astype(BF16)
    rec_ref[0] = _dot(h, w_ref[:, d:])


def rg_in(xc, xl, scc, shc, scl, shl, w):
    bsz, nc, d = xc.shape
    tm = RG_TILE
    nct = nc // tm
    nt = nct + xl.shape[1] // tm
    row = pl.BlockSpec((1, tm, d), lambda b, i: (b, i, 0))
    vec = pl.BlockSpec((1, 1, d), lambda b, i: (b, 0, 0))
    return pl.pallas_call(
        functools.partial(_rg_in_kernel, nct=nct),
        grid=(bsz, nt),
        in_specs=[pl.BlockSpec((1, tm, d), lambda b, i: (b, jnp.minimum(i, nct - 1), 0)),
                  pl.BlockSpec((1, tm, d), lambda b, i: (b, jnp.maximum(i - nct, 0), 0)),
                  vec, vec, vec, vec, pl.BlockSpec(w.shape, lambda b, i: (0, 0))],
        out_specs=[row, row],
        out_shape=[jax.ShapeDtypeStruct((bsz, nt * tm, d), BF16), jax.ShapeDtypeStruct((bsz, nt * tm, d), F32)],
        compiler_params=_cparams(("arbitrary", "arbitrary")),
        name="rg_in",
    )(xc, xl, scc, shc, scl, shl, w)


def _rg_scan_kernel(cf_ref, pf_ref, nf_ref, cb_ref, pb_ref, nb_ref, cw_ref, cbias_ref, wcat_ref, bcat_ref,
                    lam_ref, hf_ref, hb_ref, u_scr, af_scr, bf_scr, ab_scr, bb_scr, hf_scr, hb_scr, carry,
                    *, nct):
    bsz, tl, d = cf_ref.shape
    bw = d // RG_BLOCKS
    i = pl.program_id(0)
    n = pl.num_programs(0)

    @pl.when(i == 0)
    def _():
        carry[...] = jnp.zeros_like(carry)

    tiles = (i, jnp.where(i < nct, nct - 1 - i, n - 1 + nct - i))
    srcs = ((cf_ref, pf_ref, nf_ref, af_scr, bf_scr), (cb_ref, pb_ref, nb_ref, ab_scr, bb_scr))
    cw = cw_ref[...]
    for dr in range(2):
        cur, prev, nxt, a_scr, b_scr = srcs[dr]
        tile = tiles[dr]
        first = (tile == 0) | (tile == nct)
        last = (tile == nct - 1) | (tile == n - 1)

        def conv_b(b, c, cur=cur, prev=prev, nxt=nxt, first=first, last=last):
            x = cur[b]
            e = jnp.concatenate([jnp.where(first, 0.0, prev[b]), x, jnp.where(last, 0.0, nxt[b])], axis=0)
            m = tl + 2 * RG_HALO
            sh = lambda k: pltpu.roll(e, k % m, 0)[RG_HALO:RG_HALO + tl]
            u_scr[b] = cw[0:1] * sh(2) + cw[1:2] * sh(1) + cw[2:3] * x + cw[3:4] * sh(-1) + cbias_ref[...]
            return c

        lax.fori_loop(0, bsz, conv_b, 0)
        u = u_scr[...].reshape(bsz * tl, d)
        ub = u.astype(BF16)
        lam = lam_ref[dr:dr + 1, :]
        log_sig = jnp.minimum(lam, 0.0) - jnp.log(1.0 + jnp.exp(-jnp.abs(lam)))
        for kb in range(RG_BLOCKS):
            sl = slice(kb * bw, (kb + 1) * bw)
            y = _dot(ub[:, sl], wcat_ref[dr, kb]) + bcat_ref[dr, kb]
            r = _sigmoid(y[:, :bw])
            g = _sigmoid(y[:, bw:])
            a = jnp.exp(RG_C * r * log_sig[:, sl])
            a_scr[:, :, sl] = a.reshape(bsz, tl, bw)
            b_scr[:, :, sl] = (jnp.sqrt(1.0 - a * a) * (g * u[:, sl])).reshape(bsz, tl, bw)

    def step(s, hs):
        hf, hb = hs
        tb = tl - 1 - s
        hf = af_scr[:, s, :] * hf + bf_scr[:, s, :]
        hf_scr[:, s, :] = hf
        hb = ab_scr[:, tb, :] * hb + bb_scr[:, tb, :]
        hb_scr[:, tb, :] = hb
        return hf, hb

    hf, hb = lax.fori_loop(0, tl, step, (carry[0], carry[1]), unroll=2)
    carry[0] = hf
    carry[1] = hb
    hf_ref[...] = hf_scr[...].astype(BF16)
    hb_ref[...] = hb_scr[...].astype(BF16)


def rg_scan(rec, n_ctx, conv_w, conv_b, wcat, bcat, lam):
    bsz, n, d = rec.shape
    tl = RG_STEPS
    nt = n // tl
    nct = n_ctx // tl
    per = tl // RG_HALO
    fmap = lambda i: i
    bmap = lambda i: jnp.where(i < nct, nct - 1 - i, nt - 1 + nct - i)

    def views(tmap):
        return [pl.BlockSpec((bsz, tl, d), lambda i: (0, tmap(i), 0)),
                pl.BlockSpec((bsz, RG_HALO, d), lambda i: (0, jnp.maximum(tmap(i) * per - 1, 0), 0)),
                pl.BlockSpec((bsz, RG_HALO, d), lambda i: (0, jnp.minimum((tmap(i) + 1) * per, n // RG_HALO - 1), 0))]

    full = lambda arr: pl.BlockSpec(arr.shape, lambda i: (0,) * arr.ndim)
    cb = conv_b[None]
    slab = pltpu.VMEM((bsz, tl, d), F32)
    return pl.pallas_call(
        functools.partial(_rg_scan_kernel, nct=nct),
        grid=(nt,),
        in_specs=views(fmap) + views(bmap) + [full(conv_w), full(cb), full(wcat), full(bcat), full(lam)],
        out_specs=[pl.BlockSpec((bsz, tl, d), lambda i: (0, fmap(i), 0)),
                   pl.BlockSpec((bsz, tl, d), lambda i: (0, bmap(i), 0))],
        out_shape=[jax.ShapeDtypeStruct((bsz, n, d), BF16)] * 2,
        scratch_shapes=[slab] * 7 + [pltpu.VMEM((2, bsz, d), F32)],
        compiler_params=_cparams(("arbitrary",)),
        name="rg_scan",
    )(rec, rec, rec, rec, rec, rec, conv_w, cb, wcat, bcat, lam)


def _route_kernel(x_ref, sc_ref, sh_ref, wh_ref, wl_ref, bias_ref, gates_ref, rank_ref, rank_t_ref):
    t = x_ref[0] * (1.0 + sc_ref[0]) + sh_ref[0]
    th = t.astype(BF16)
    tl = (t - th.astype(F32)).astype(BF16)
    logits = _dot_nt(wh_ref[...], th) + (_dot_nt(wh_ref[...], tl) + _dot_nt(wl_ref[...], th))
    scores = jax.nn.sigmoid(logits)
    sel = scores + bias_ref[...]
    ninf = -jnp.inf
    sub = lax.broadcasted_iota(jnp.int32, (EPG, sel.shape[1]), 0)
    gscore = []
    for g in range(N_GROUPS):
        slab = sel[g * EPG:(g + 1) * EPG]
        m1 = jnp.max(slab, axis=0, keepdims=True)
        first = jnp.min(jnp.where(slab == m1, sub, EPG), axis=0, keepdims=True)
        m2 = jnp.max(jnp.where(sub == first, ninf, slab), axis=0, keepdims=True)
        gscore.append(m1 + m2)
    slabs = []
    for g in range(N_GROUPS):
        beaten = jnp.zeros(gscore[g].shape, F32)
        for o in range(N_GROUPS):
            if o != g:
                wins = (gscore[o] >= gscore[g]) if o < g else (gscore[o] > gscore[g])
                beaten = beaten + jnp.where(wins, 1.0, 0.0)
        keep = jnp.broadcast_to(beaten < TOPK_GROUPS, (EPG, sel.shape[1]))
        slabs.append(jnp.where(keep, sel[g * EPG:(g + 1) * EPG], ninf))
    v = jnp.concatenate(slabs, axis=0)
    eidx = lax.broadcasted_iota(jnp.int32, v.shape, 0)
    w = jnp.zeros(v.shape, F32)
    chosen = jnp.zeros(v.shape, F32)
    for _ in range(TOP_K):
        m = jnp.max(v, axis=0, keepdims=True)
        first = jnp.min(jnp.where(v == m, eidx, N_EXPERTS), axis=0, keepdims=True)
        hit = eidx == first
        w = jnp.where(hit, scores, w)
        chosen = jnp.where(hit, 1.0, chosen)
        v = jnp.where(hit, ninf, v)
    gates = w / jnp.sum(w, axis=0, keepdims=True) * ROUTED_SCALE
    tm = v.shape[1]
    upper = jnp.where(lax.broadcasted_iota(jnp.int32, (tm, tm), 0) < lax.broadcasted_iota(jnp.int32, (tm, tm), 1),
                      1.0, 0.0).astype(BF16)
    rank = jnp.where(chosen > 0.0, _dot(chosen.astype(BF16), upper), -1.0)
    pad = LANES - N_EXPERTS
    gates_ref[...] = jnp.concatenate([gates, jnp.zeros((pad, tm), F32)], axis=0).T
    rank_p = jnp.concatenate([rank, jnp.full((pad, tm), -1.0, F32)], axis=0)
    rank_ref[...] = rank_p.T.astype(BF16)
    rank_t_ref[...] = rank_p.astype(BF16)


MOE_WIN = 256
SEG_ALIGN = 16
MOE_TILE = 512
WIN_ROWS = -(-(MOE_WIN * TOP_K + N_EXPERTS * (SEG_ALIGN - 1)) // 256) * 256
SEG_SIZES = tuple(SEG_ALIGN << k for k in reversed(range((MOE_WIN // SEG_ALIGN).bit_length())))
WAIT_SIZES = tuple(SEG_ALIGN << k for k in reversed(range((WIN_ROWS // SEG_ALIGN).bit_length())))


def route(x, sc, sh, rw_hi, rw_lo, bias):
    bsz, n, d = x.shape
    tm = MOE_WIN
    nt = n // tm
    full = lambda arr: pl.BlockSpec(arr.shape, lambda b, i: (0,) * arr.ndim)
    vec = pl.BlockSpec((1, 1, d), lambda b, i: (b, 0, 0))
    tok = pl.BlockSpec((tm, LANES), lambda b, i: (b * nt + i, 0))
    return pl.pallas_call(
        _route_kernel,
        grid=(bsz, nt),
        in_specs=[pl.BlockSpec((1, tm, d), lambda b, i: (b, i, 0)), vec, vec, full(rw_hi), full(rw_lo),
                  full(bias)],
        out_specs=[tok, tok, pl.BlockSpec((LANES, tm), lambda b, i: (0, b * nt + i))],
        out_shape=[jax.ShapeDtypeStruct((bsz * n, LANES), F32), jax.ShapeDtypeStruct((bsz * n, LANES), BF16),
                   jax.ShapeDtypeStruct((LANES, bsz * n), BF16)],
        compiler_params=_cparams(("arbitrary", "arbitrary")),
        name="route",
    )(x, sc, sh, rw_hi, rw_lo, bias)


def moe_plan(rank_tm, nw):
    cnt = (rank_tm[:, :N_EXPERTS] >= 0).reshape(nw, MOE_WIN, N_EXPERTS).sum(axis=1).astype(jnp.int32)
    pc = (cnt + SEG_ALIGN - 1) // SEG_ALIGN * SEG_ALIGN
    lo = jnp.cumsum(pc, axis=1) - pc
    used = pc.sum(axis=0)
    tot = (used + MOE_TILE - 1) // MOE_TILE * MOE_TILE
    ends = jnp.cumsum(tot)
    go = (ends - tot)[None, :] + jnp.cumsum(pc, axis=0) - pc
    n_tiles = (nw * MOE_WIN * TOP_K + nw * N_EXPERTS * (SEG_ALIGN - 1)) // MOE_TILE + N_EXPERTS
    tile_expert = jnp.sum(jnp.arange(n_tiles, dtype=jnp.int32)[:, None] >= (ends // MOE_TILE)[None, :], axis=1)
    col = lambda a: jnp.pad(a.astype(F32), ((0, 0), (0, LANES - N_EXPERTS)))
    return dict(
        lo=lo.reshape(-1), pc=pc.reshape(-1), go=go.reshape(-1),
        tail_pc=tot - used, tail_go=ends - tot + used, used=pc.sum(axis=1),
        lo_row=col(lo)[:, None, :], pc_row=col(pc)[:, None, :],
        lo_col=col(lo)[:, :, None], pc_col=col(pc)[:, :, None],
        tile_expert=jnp.minimum(tile_expert, N_EXPERTS - 1).astype(jnp.int32),
        n_used=(ends[-1] // MOE_TILE).astype(jnp.int32).reshape(1), n_tiles=n_tiles)


def _segment_copies(lo_s, pc_s, go_s, w, local_ref, sorted_hbm, sem, to_sorted, wait):
    def body(e, carry):
        idx = w * N_EXPERTS + e
        lo = 0 if lo_s is None else lo_s[idx]
        pc = pc_s[idx]
        go = go_s[idx]
        off = jnp.int32(0)
        for size in SEG_SIZES:
            bit = pc & size

            @pl.when(bit != 0)
            def _(off=off, size=size):
                loc = local_ref.at[pl.ds(pl.multiple_of(lo + off, SEG_ALIGN), size), :]
                glo = sorted_hbm.at[pl.ds(pl.multiple_of(go + off, SEG_ALIGN), size), :]
                cp = pltpu.make_async_copy(loc, glo, sem) if to_sorted else pltpu.make_async_copy(glo, loc, sem)
                if wait:
                    cp.wait()
                else:
                    cp.start()

            off = off + bit
        return carry

    lax.fori_loop(0, N_EXPERTS, body, 0)


def _wait_rows(rows, local_ref, sorted_hbm, sem, to_sorted):
    for size in WAIT_SIZES:
        @pl.when((rows & size) != 0)
        def _(size=size):
            loc = local_ref.at[pl.ds(0, size), :]
            glo = sorted_hbm.at[pl.ds(0, size), :]
            cp = pltpu.make_async_copy(loc, glo, sem) if to_sorted else pltpu.make_async_copy(glo, loc, sem)
            cp.wait()


def _dispatch_kernel(lo_s, pc_s, go_s, tpc_s, tgo_s, used_s, x_ref, sc_ref, sh_ref, rank_t_ref, lo_ref,
                     pc_ref, sorted_hbm, rows_ref, sem):
    w = pl.program_id(0) * pl.num_programs(1) + pl.program_id(1)

    @pl.when(w == 0)
    def _():
        rows_ref[0:MOE_TILE, :] = jnp.zeros((MOE_TILE, rows_ref.shape[1]), BF16)
        _segment_copies(None, tpc_s, tgo_s, 0, rows_ref, sorted_hbm, sem.at[0], True, False)
        _segment_copies(None, tpc_s, tgo_s, 0, rows_ref, sorted_hbm, sem.at[0], True, True)

    t = (x_ref[0] * (1.0 + sc_ref[0]) + sh_ref[0]).astype(BF16)
    lo = lo_ref[0]
    hi = lo + pc_ref[0]
    chunk = 512
    for c in range(WIN_ROWS // chunk):
        r = (lax.broadcasted_iota(jnp.int32, (chunk, LANES), 0) + c * chunk).astype(F32)
        owner = (r >= lo) & (r < hi)
        seg_lo = jnp.sum(jnp.where(owner, lo, 0.0), axis=1, keepdims=True)
        rank_of = _dot(jnp.where(owner, 1.0, 0.0).astype(BF16), rank_t_ref[...])
        pick = jnp.where(rank_of == r[:, 0:1] - seg_lo, 1.0, 0.0).astype(BF16)
        rows_ref[c * chunk:(c + 1) * chunk, :] = _dot(pick, t).astype(BF16)
    _segment_copies(lo_s, pc_s, go_s, w, rows_ref, sorted_hbm, sem.at[0], True, False)
    _wait_rows(used_s[w], rows_ref, sorted_hbm, sem.at[0], True)


def moe_dispatch(x, sc, sh, rank_t, plan, n_rows):
    bsz, n, d = x.shape
    nwb = n // MOE_WIN
    vec = pl.BlockSpec((1, 1, d), lambda b, i, *_: (b, 0, 0))
    meta = pl.BlockSpec((1, 1, LANES), lambda b, i, *_: (b * nwb + i, 0, 0))
    return pl.pallas_call(
        _dispatch_kernel,
        grid_spec=pltpu.PrefetchScalarGridSpec(
            num_scalar_prefetch=6,
            grid=(bsz, nwb),
            in_specs=[pl.BlockSpec((1, MOE_WIN, d), lambda b, i, *_: (b, i, 0)), vec, vec,
                      pl.BlockSpec((LANES, MOE_WIN), lambda b, i, *_: (0, b * nwb + i)), meta, meta],
            out_specs=pl.BlockSpec(memory_space=pl.ANY),
            scratch_shapes=[pltpu.VMEM((WIN_ROWS, d), BF16), pltpu.SemaphoreType.DMA((1,))]),
        out_shape=jax.ShapeDtypeStruct((n_rows, d), BF16),
        compiler_params=_cparams(("arbitrary", "arbitrary")),
        name="moe_dispatch",
    )(plan["lo"], plan["pc"], plan["go"], plan["tail_pc"], plan["tail_go"], plan["used"], x, sc, sh, rank_t,
      plan["lo_row"], plan["pc_row"])


def _expert_kernel(te_s, nu_s, x_ref, wg_ref, wu_ref, wd_ref, o_ref):
    @pl.when(pl.program_id(0) < nu_s[0])
    def _():
        wg = wg_ref[0, 0].astype(BF16)
        wu = wu_ref[0, 0].astype(BF16)
        wd = wd_ref[0, 0].astype(BF16)
        part = MOE_TILE // 2
        for s in range(2):
            x = x_ref[s * part:(s + 1) * part, :]
            h = _silu(_dot(x, wg)) * _dot(x, wu)
            o_ref[s * part:(s + 1) * part, :] = _dot(h.astype(BF16), wd).astype(BF16)


def moe_experts(xs, plan, layer, wg, wu, wd):
    n_rows, d = xs.shape
    de = wg.shape[3]
    tile = lambda i, te, nu: jnp.minimum(i, nu[0] - 1)
    row = pl.BlockSpec((MOE_TILE, d), lambda i, te, nu: (tile(i, te, nu), 0))
    return pl.pallas_call(
        _expert_kernel,
        grid_spec=pltpu.PrefetchScalarGridSpec(
            num_scalar_prefetch=2,
            grid=(plan["n_tiles"],),
            in_specs=[row,
                      pl.BlockSpec((1, 1, d, de), lambda i, te, nu: (layer, te[tile(i, te, nu)], 0, 0)),
                      pl.BlockSpec((1, 1, d, de), lambda i, te, nu: (layer, te[tile(i, te, nu)], 0, 0)),
                      pl.BlockSpec((1, 1, de, d), lambda i, te, nu: (layer, te[tile(i, te, nu)], 0, 0))],
            out_specs=row),
        out_shape=jax.ShapeDtypeStruct((n_rows, d), BF16),
        compiler_params=_cparams(("arbitrary",)),
        name="moe_experts",
    )(plan["tile_expert"], plan["n_used"], xs, wg, wu, wd)


def _combine_kernel(lo_s, pc_s, go_s, used_s, x_ref, sc_ref, sh_ref, gf_ref, gates_ref, rank_ref, lo_ref, pc_ref,
                    sorted_hbm, sg_ref, su_ref, sd_ref, lg_ref, lb_ref, o_ref, rows_ref, sem):
    first = (pl.program_id(0) == 0) & (pl.program_id(1) == 0)

    @pl.when(first)
    def _():
        rows_ref[...] = jnp.zeros_like(rows_ref)

    w = pl.program_id(0) * pl.num_programs(1) + pl.program_id(1)
    _segment_copies(lo_s, pc_s, go_s, w, rows_ref, sorted_hbm, sem.at[0], False, False)
    x = x_ref[0]
    t = (x * (1.0 + sc_ref[0]) + sh_ref[0]).astype(BF16)
    shared = _dot((_silu(_dot(t, sg_ref[0].astype(BF16))) * _dot(t, su_ref[0].astype(BF16))).astype(BF16),
                  sd_ref[0].astype(BF16))
    r = lax.broadcasted_iota(jnp.int32, (LANES, WIN_ROWS), 1).astype(F32)
    lo = lo_ref[0]
    owner = (r >= lo) & (r < lo + pc_ref[0])
    seg_lo = jnp.sum(jnp.where(owner, lo, 0.0), axis=0, keepdims=True)
    owner_b = jnp.where(owner, 1.0, 0.0).astype(BF16)
    rank_of = _dot(rank_ref[...], owner_b)
    gate_of = _dot(gates_ref[...].astype(BF16), owner_b)
    weights = jnp.where(rank_of == r[0:1, :] - seg_lo, gate_of, 0.0).astype(BF16)
    _wait_rows(used_s[w], rows_ref, sorted_hbm, sem.at[0], False)
    routed = _dot(weights, rows_ref[...])
    o_ref[0] = _layer_norm(DN_ALPHA * x + gf_ref[0] * (shared + routed), lg_ref[...], lb_ref[...])


def moe_combine(x, sc, sh, gf, gates, rank_tm, plan, ys, layer, sg, su, sd, ln_g, ln_b):
    bsz, n, d = x.shape
    nwb = n // MOE_WIN
    row = pl.BlockSpec((1, MOE_WIN, d), lambda b, i, *_: (b, i, 0))
    vec = pl.BlockSpec((1, 1, d), lambda b, i, *_: (b, 0, 0))
    tok = pl.BlockSpec((MOE_WIN, LANES), lambda b, i, *_: (b * nwb + i, 0))
    meta = pl.BlockSpec((1, LANES, 1), lambda b, i, *_: (b * nwb + i, 0, 0))
    lnv = pl.BlockSpec((1, d), lambda b, i, *_: (0, 0))
    return pl.pallas_call(
        _combine_kernel,
        grid_spec=pltpu.PrefetchScalarGridSpec(
            num_scalar_prefetch=4,
            grid=(bsz, nwb),
            in_specs=[row, vec, vec, vec, tok, tok, meta, meta,
                      pl.BlockSpec(memory_space=pl.ANY),
                      pl.BlockSpec((1, d, sg.shape[2]), lambda b, i, *_: (layer, 0, 0)),
                      pl.BlockSpec((1, d, su.shape[2]), lambda b, i, *_: (layer, 0, 0)),
                      pl.BlockSpec((1, sd.shape[1], d), lambda b, i, *_: (layer, 0, 0)),
                      lnv, lnv],
            out_specs=row,
            scratch_shapes=[pltpu.VMEM((WIN_ROWS, d), BF16), pltpu.SemaphoreType.DMA((1,))]),
        out_shape=jax.ShapeDtypeStruct(x.shape, F32),
        compiler_params=_cparams(("arbitrary", "arbitrary")),
        name="moe_combine",
    )(plan["lo"], plan["pc"], plan["go"], plan["used"], x, sc, sh, gf, gates, rank_tm, plan["lo_col"],
      plan["pc_col"], ys,
      sg, su, sd, ln_g[None], ln_b[None])


def kernel(x, c, ctx, c_ctx, w_mod, b_mod, ln1_g, ln1_b, ln2_g, ln2_b, ab_w_in, ab_w_out, hy_short_w, hy_short_b, hy_fw1, hy_fb1, hy_fw2, hy_fb2, hy_fw3, hy_fb3, hy_fw4, hy_freq, hy_decay, hy_skip, da_lq1, da_lk1, da_lq2, da_lk2, da_subln_g, rg_w_in, rg_w_out, rg_conv_w, rg_conv_b, rg_wa, rg_ba, rg_wx, rg_bx, rg_lambda, router_w, router_bias, ex_w_gate, ex_w_up, ex_w_down, sh_w_gate, sh_w_up, sh_w_down):
    return forward(dict(locals()))


def split_mod(mod_l, bsz, d):
    ml = [mod_l[:bsz, k * d:(k + 1) * d][:, None, :] for k in range(N_MOD)]
    mc = [jnp.broadcast_to(mod_l[bsz, k * d:(k + 1) * d][None, None, :], (bsz, 1, d)) for k in range(N_MOD)]
    return ml, mc


def mixer_even(p, l, xs, cs, ml, mc, need_ctx):
    e = l // 2
    n_lat, n_ctx = xs.shape[1], cs.shape[1]
    lam_init = 0.8 - 0.6 * math.exp(-0.3 * l)
    w_in = p["ab_w_in"][e].astype(BF16)
    w_out = p["ab_w_out"][e].astype(BF16)
    p_lat = ab_in(xs, ml[1], ml[0], w_in)
    p_ctx = ab_in(cs, mc[1], mc[0], w_in)
    filt_p = tuple(p[k][e] for k in ("hy_fw1", "hy_fb1", "hy_fw2", "hy_fb2", "hy_fw3", "hy_fb3", "hy_fw4",
                                     "hy_freq", "hy_decay"))
    lam_params = jnp.pad(jnp.stack([p["da_lq1"][e], p["da_lk1"][e], p["da_lq2"][e], p["da_lk2"][e]]),
                         ((0, 0), (0, LANES - DA_HD)))
    rope_lat = rope_tables(n_lat)
    rope_ctx = tuple(t[:n_ctx] for t in rope_lat)

    def hyena(proj, n):
        dft = dft_tables(n)
        hs, hd = hyena_filter_parts(n, *filt_p)
        tabs = filter_spectrum(hs, hd, dft[0], dft[1])
        return hyena_mixer(proj, p["hy_short_w"][e], p["hy_short_b"][e], p["hy_skip"][e], tabs, *dft)

    g1, b1 = p["ln1_g"][l], p["ln1_b"][l]
    sub_g = p["da_subln_g"][e]
    y_lat = hyena(p_lat, n_lat)
    o_lat = diff_attention(p_lat, p_ctx, rope_lat, lam_params, sub_g, lam_init, True)
    xs_new = out_proj_norm(y_lat, o_lat, w_out, xs, ml[2], g1, b1)
    if need_ctx:
        y_ctx = hyena(p_ctx, n_ctx)
        o_ctx = diff_attention(p_ctx, None, rope_ctx, lam_params, sub_g, lam_init, False)
        cs = out_proj_norm(y_ctx, o_ctx, w_out, cs, mc[2], g1, b1)
    return xs_new, cs


def mixer_odd(p, l, xs, cs, ml, mc, need_ctx):
    o = l // 2
    w_in = p["rg_w_in"][o].astype(BF16)
    w_out = p["rg_w_out"][o].astype(BF16)
    n_ctx = cs.shape[1]
    gate, rec = rg_in(cs, xs, mc[1], mc[0], ml[1], ml[0], w_in)
    wcat = jnp.concatenate([p["rg_wa"][o], p["rg_wx"][o]], axis=-1).astype(BF16)
    split = lambda v: v.reshape(2, RG_BLOCKS, 1, -1)
    bcat = jnp.concatenate([split(p["rg_ba"][o]), split(p["rg_bx"][o])], axis=-1)
    hf, hb = rg_scan(rec, n_ctx, p["rg_conv_w"][o], p["rg_conv_b"][o], wcat, bcat, p["rg_lambda"][o])
    g1, b1 = p["ln1_g"][l], p["ln1_b"][l]
    xs = rg_out(gate, hf, hb, n_ctx, w_out, xs, ml[2], g1, b1)
    if need_ctx:
        cs = rg_out(gate, hf, hb, 0, w_out, cs, mc[2], g1, b1)
    return xs, cs


def moe_block(p, l, h, m):
    rw_t = p["router_w"][l].T
    rw_hi = rw_t.astype(BF16)
    rw_lo = (rw_t - rw_hi.astype(F32)).astype(BF16)
    gates, rank_tm, rank_t = route(h, m[4], m[3], rw_hi, rw_lo, p["router_bias"][l][:, None])
    plan = moe_plan(rank_tm, gates.shape[0] // MOE_WIN)
    xs = moe_dispatch(h, m[4], m[3], rank_t, plan, plan["n_tiles"] * MOE_TILE)
    ys = moe_experts(xs, plan, l, p["ex_w_gate"], p["ex_w_up"], p["ex_w_down"])
    return moe_combine(h, m[4], m[3], m[5], gates, rank_tm, plan, ys, l, p["sh_w_gate"], p["sh_w_up"],
                       p["sh_w_down"], p["ln2_g"][l], p["ln2_b"][l])


def run_layer(p, l, xs, cs, mod_l):
    bsz, _, d = xs.shape
    need_ctx = l < DEPTH - 1
    ml, mc = split_mod(mod_l, bsz, d)
    mixer = mixer_even if l % 2 == 0 else mixer_odd
    xs, cs = mixer(p, l, xs, cs, ml, mc, need_ctx)
    xs = moe_block(p, l, xs, ml)
    if need_ctx:
        cs = moe_block(p, l, cs, mc)
    return xs, cs


def conditioning_rows(c, c_ctx):
    bsz, d = c.shape
    rows = 8 * ((bsz + 1 + 7) // 8)
    return jnp.zeros((rows, d), F32).at[:bsz].set(c).at[bsz].set(c_ctx)


def forward(p):
    mod = modulation(conditioning_rows(p["c"], p["c_ctx"]), p["w_mod"], p["b_mod"])
    xs, cs = p["x"], p["ctx"]
    for l in range(DEPTH):
        xs, cs = run_layer(p, l, xs, cs, mod[l])
    return xs
```

```python
import functools
import math

import jax
import jax.numpy as jnp
from jax import lax
from jax.experimental import pallas as pl
from jax.experimental.pallas import tpu as pltpu

F32 = jnp.float32
BF16 = jnp.bfloat16

DEPTH = 4
GRID_W = 64
N_MOD = 6
LN_EPS = 1e-5
DN_ALPHA = (2.0 * DEPTH) ** 0.25

HY_ORDER = 2
HY_EMB = 33
HY_BANDS = (HY_EMB - 1) // 2
DA_HEADS = 4
DA_HD = 64
DA_VD = 2 * DA_HD
ROPE_BASE = 10000.0
ROPE_NF = DA_HD // 4
RG_BLOCKS = 8
RG_C = 8.0
N_EXPERTS = 64
N_GROUPS = 8
EPG = N_EXPERTS // N_GROUPS
TOPK_GROUPS = 4
TOP_K = 8
ROUTED_SCALE = 2.5

LANES = 128
VMEM_LIMIT = 56 * 1024 * 1024


def _cparams(sem):
    return pltpu.CompilerParams(dimension_semantics=sem, vmem_limit_bytes=VMEM_LIMIT)


def _dot(a, b):
    return jnp.dot(a, b, preferred_element_type=F32)


def _dot_nt(a, b):
    return lax.dot_general(a, b, (((1,), (1,)), ((), ())), preferred_element_type=F32)


def _silu(x):
    return x * jax.nn.sigmoid(x)


def _sigmoid(x):
    return 0.5 * jnp.tanh(0.5 * x) + 0.5


def _layer_norm(r, g, b):
    mu = jnp.mean(r, axis=-1, keepdims=True)
    d = r - mu
    var = jnp.mean(d * d, axis=-1, keepdims=True)
    return d * lax.rsqrt(var + LN_EPS) * g + b


def _mod_kernel(cc_ref, w_ref, b_ref, o_ref):
    s = _silu(cc_ref[...])
    o_ref[0] = _dot(s.astype(BF16), w_ref[0].astype(BF16)) + b_ref[0]


def modulation(cc, w_mod, b_mod):
    r, d = cc.shape
    n = w_mod.shape[-1]
    tn = 1536
    return pl.pallas_call(
        _mod_kernel,
        grid=(DEPTH, n // tn),
        in_specs=[pl.BlockSpec((r, d), lambda l, j: (0, 0)),
                  pl.BlockSpec((1, d, tn), lambda l, j: (l, 0, j)),
                  pl.BlockSpec((1, 1, tn), lambda l, j: (l, 0, j))],
        out_specs=pl.BlockSpec((1, r, tn), lambda l, j: (l, 0, j)),
        out_shape=jax.ShapeDtypeStruct((DEPTH, r, n), F32),
        compiler_params=_cparams(("arbitrary", "arbitrary")),
        name="modulation",
    )(cc, w_mod, b_mod.reshape(DEPTH, 1, n))


def _ab_in_kernel(x_ref, sc_ref, sh_ref, w_ref, o_ref, *, tn):
    h = (x_ref[0] * (1.0 + sc_ref[0]) + sh_ref[0]).astype(BF16)
    for j in range(w_ref.shape[1] // tn):
        o_ref[0, :, j * tn:(j + 1) * tn] = _dot(h, w_ref[:, j * tn:(j + 1) * tn]).astype(BF16)


def ab_in(x, sc, sh, w):
    bsz, n, d = x.shape
    nout = w.shape[1]
    tm = min(n, 512)
    return pl.pallas_call(
        functools.partial(_ab_in_kernel, tn=1024),
        grid=(bsz, n // tm),
        in_specs=[pl.BlockSpec((1, tm, d), lambda b, i: (b, i, 0)),
                  pl.BlockSpec((1, 1, d), lambda b, i: (b, 0, 0)),
                  pl.BlockSpec((1, 1, d), lambda b, i: (b, 0, 0)),
                  pl.BlockSpec((d, nout), lambda b, i: (0, 0))],
        out_specs=pl.BlockSpec((1, tm, nout), lambda b, i: (b, i, 0)),
        out_shape=jax.ShapeDtypeStruct((bsz, n, nout), BF16),
        compiler_params=_cparams(("arbitrary", "arbitrary")),
        name="ab_in",
    )(x, sc, sh, w)


def _filter_kernel(z_ref, w1_ref, b1_ref, w2_ref, b2_ref, w3_ref, b3_ref, w4_ref, fr_ref, dec_ref,
                   hs_ref, hd_ref):
    hp = lax.Precision.HIGHEST
    z = z_ref[...]
    fr = fr_ref[...]
    h = jnp.sin(fr * (jnp.dot(z, w1_ref[...], precision=hp, preferred_element_type=F32) + b1_ref[...]))
    h = jnp.sin(fr * (jnp.dot(h, w2_ref[...], precision=hp, preferred_element_type=F32) + b2_ref[...]))
    h = jnp.sin(fr * (jnp.dot(h, w3_ref[...], precision=hp, preferred_element_type=F32) + b3_ref[...]))
    t = z[:, 0:1]
    f = jnp.dot(h, w4_ref[...], precision=hp, preferred_element_type=F32) * jnp.exp(-t * jnp.abs(dec_ref[...]))
    w = f.shape[1] // (2 * HY_ORDER)
    for o in range(HY_ORDER):
        hf = f[:, (2 * o) * w:(2 * o + 1) * w]
        hb = f[:, (2 * o + 1) * w:(2 * o + 2) * w]
        hs_ref[o] = (hf + hb).astype(BF16)
        hd_ref[o] = (hb - hf).astype(BF16)


def hyena_filter_parts(n, w1, b1, w2, b2, w3, b3, w4, freq, decay):
    t = jnp.linspace(0.0, 1.0, n, dtype=F32)[:, None]
    f = jnp.linspace(1e-4, HY_BANDS - 1, HY_BANDS, dtype=F32)
    ang = (2.0 * math.pi / n) * jnp.arange(n, dtype=F32)[:, None] * f[None, :]
    z = jnp.concatenate([t, jnp.cos(ang), -jnp.sin(ang)], axis=-1)
    z = jnp.pad(z, ((0, 0), (0, LANES - HY_EMB)))
    w1p = jnp.pad(w1, ((0, LANES - HY_EMB), (0, 0)))
    hid = w1.shape[1]
    nf = w4.shape[1]
    w = nf // (2 * HY_ORDER)
    tm = min(n, 512)
    full = lambda shape: pl.BlockSpec(shape, lambda i: (0,) * len(shape))
    return pl.pallas_call(
        _filter_kernel,
        grid=(n // tm,),
        in_specs=[pl.BlockSpec((tm, LANES), lambda i: (i, 0)),
                  full((LANES, hid)), full((1, hid)), full((hid, hid)), full((1, hid)),
                  full((hid, hid)), full((1, hid)), full((hid, nf)), full((1, hid)), full((1, nf))],
        out_specs=[pl.BlockSpec((HY_ORDER, tm, w), lambda i: (0, i, 0)),
                   pl.BlockSpec((HY_ORDER, tm, w), lambda i: (0, i, 0))],
        out_shape=[jax.ShapeDtypeStruct((HY_ORDER, n, w), BF16)] * 2,
        compiler_params=_cparams(("arbitrary",)),
        name="hyena_filter",
    )(z, w1p, b1[None], w2, b2[None], w3, b3[None], w4, freq[None], decay[None])


def dft_tables(n):
    k = (jnp.arange(n, dtype=jnp.int32)[:, None] * jnp.arange(n, dtype=jnp.int32)[None, :]) % (2 * n)
    ang = k.astype(F32) * (math.pi / n)
    c = jnp.cos(ang)
    s = jnp.sin(ang)
    alt = jnp.where(jnp.arange(n) % 2 == 0, 1.0, -1.0).astype(F32)
    s_fwd = s.at[0, :].set(alt)
    s_inv = s.at[:, 0].set(alt)
    return c.astype(BF16), s_fwd.astype(BF16), s_inv.astype(BF16)


def _spectrum_kernel(hs_ref, hd_ref, c_ref, s_ref, o_ref, *, n):
    f0 = pl.program_id(1) * c_ref.shape[0]
    hs = hs_ref[0]
    hr = _dot(c_ref[...], hs)
    hq = _dot(s_ref[...], hs)
    hi = _dot(s_ref[...], hd_ref[0])
    row = lax.broadcasted_iota(jnp.int32, hr.shape, 0) + f0
    dc = row == 0
    inv = 1.0 / (2 * n)
    o_ref[0, 0] = jnp.where(dc, hr * inv, hr * (2.0 * inv))
    o_ref[0, 1] = jnp.where(dc, 0.0, hi * (2.0 * inv))
    o_ref[0, 2] = jnp.where(dc, hq * inv, hr * (2.0 * inv))


def filter_spectrum(hs, hd, c, s_fwd):
    _, n, w = hs.shape
    tf = min(n, 256)
    return pl.pallas_call(
        functools.partial(_spectrum_kernel, n=n),
        grid=(HY_ORDER, n // tf),
        in_specs=[pl.BlockSpec((1, n, w), lambda o, f: (o, 0, 0)),
                  pl.BlockSpec((1, n, w), lambda o, f: (o, 0, 0)),
                  pl.BlockSpec((tf, n), lambda o, f: (f, 0)),
                  pl.BlockSpec((tf, n), lambda o, f: (f, 0))],
        out_specs=pl.BlockSpec((1, 3, tf, w), lambda o, f: (o, 0, f, 0)),
        out_shape=jax.ShapeDtypeStruct((HY_ORDER, 3, n, w), F32),
        compiler_params=_cparams(("arbitrary", "arbitrary")),
        name="filter_spectrum",
    )(hs, hd, c, s_fwd)


def _shift_rows(u, k):
    n = u.shape[0]
    row = lax.broadcasted_iota(jnp.int32, u.shape, 0)
    r = pltpu.roll(u, k % n, 0)
    if k > 0:
        return jnp.where(row < k, 0.0, r)
    return jnp.where(row >= n + k, 0.0, r)


def _hyena_kernel(p_ref, sw_ref, sb_ref, skip_ref, c_ref, s_ref, ct_ref, st_ref, tab_ref, o_ref,
                  sig_ref, z_ref, acc_ref, *, w):
    ph = pl.program_id(1)
    f = pl.program_id(2)
    last = pl.num_programs(2) - 1

    def short_conv(lo):
        u = p_ref[0, :, lo:lo + w].astype(F32)
        cw = sw_ref[:, lo:lo + w]
        return (cw[0:1] * _shift_rows(u, 1) + cw[1:2] * u + cw[2:3] * _shift_rows(u, -1)
                + sb_ref[:, lo:lo + w])

    @pl.when((ph == 0) & (f == 0))
    def _():
        sig_ref[...] = short_conv(0).astype(BF16)

    @pl.when(f == 0)
    def _():
        acc_ref[...] = jnp.zeros_like(acc_ref)

    sig = sig_ref[...]
    p = _dot(c_ref[...], sig)
    q = _dot(s_ref[...], sig)
    hr = tab_ref[0, 0]
    hi = tab_ref[0, 1]
    hq = tab_ref[0, 2]
    zr = (p * hr + q * hi).astype(BF16)
    zq = (q * hq - p * hi).astype(BF16)
    acc_ref[...] += _dot(ct_ref[...], zr) + _dot(st_ref[...], zq)

    @pl.when((f == last) & (ph == 0))
    def _():
        z = short_conv(w) * (acc_ref[...] + skip_ref[0:1] * short_conv(0))
        z_ref[...] = z
        sig_ref[...] = z.astype(BF16)

    @pl.when((f == last) & (ph == 1))
    def _():
        o_ref[0] = (short_conv(2 * w) * (acc_ref[...] + skip_ref[1:2] * z_ref[...])).astype(BF16)


def hyena_mixer(p, short_w, short_b, skip, tabs, c, s_fwd, s_inv):
    bsz, n, _ = p.shape
    w = skip.shape[1]
    tf = min(n, 256)
    return pl.pallas_call(
        functools.partial(_hyena_kernel, w=w),
        grid=(bsz, HY_ORDER, n // tf),
        in_specs=[pl.BlockSpec((1, n, 3 * w), lambda b, o, f: (b, 0, 0)),
                  pl.BlockSpec((3, 3 * w), lambda b, o, f: (0, 0)),
                  pl.BlockSpec((1, 3 * w), lambda b, o, f: (0, 0)),
                  pl.BlockSpec((HY_ORDER, w), lambda b, o, f: (0, 0)),
                  pl.BlockSpec((tf, n), lambda b, o, f: (f, 0)),
                  pl.BlockSpec((tf, n), lambda b, o, f: (f, 0)),
                  pl.BlockSpec((n, tf), lambda b, o, f: (0, f)),
                  pl.BlockSpec((n, tf), lambda b, o, f: (0, f)),
                  pl.BlockSpec((1, 3, tf, w), lambda b, o, f: (o, 0, f, 0))],
        out_specs=pl.BlockSpec((1, n, w), lambda b, o, f: (b, 0, 0)),
        out_shape=jax.ShapeDtypeStruct((bsz, n, w), BF16),
        scratch_shapes=[pltpu.VMEM((n, w), BF16), pltpu.VMEM((n, w), F32), pltpu.VMEM((n, w), F32)],
        compiler_params=_cparams(("arbitrary", "arbitrary", "arbitrary")),
        name="hyena_mixer",
    )(p, short_w, short_b[None], skip, c, s_fwd, c, s_inv, tabs)


def rope_tables(n):
    pos = jnp.arange(n)
    rc = jnp.stack([(pos // GRID_W).astype(F32), (pos % GRID_W).astype(F32)], axis=1)
    inv = ROPE_BASE ** (-jnp.arange(ROPE_NF, dtype=F32) / ROPE_NF)
    ang = rc[:, :, None] * inv
    cos = jnp.cos(ang)
    sin = jnp.sin(ang)
    zero = jnp.zeros_like(sin)
    tile = lambda a, b: jnp.tile(jnp.stack([a, b], axis=2).reshape(n, 4 * ROPE_NF), (1, 2))
    return tile(cos, cos), tile(-sin, zero), tile(zero, sin)


def _rope(x, cos, sin_a, sin_b):
    return (x * cos + pltpu.roll(x, LANES - ROPE_NF, 1) * sin_a + pltpu.roll(x, ROPE_NF, 1) * sin_b)


def _attn_kernel(*refs, n_pre, rope, lam_init, tq):
    if n_pre:
        q_ref, k_ref, v_ref, kc_ref, vc_ref, cos_ref, sa_ref, sb_ref, lam_ref, g_ref, o_ref, kbuf = refs
    else:
        q_ref, k_ref, v_ref, cos_ref, sa_ref, sb_ref, lam_ref, g_ref, o_ref, kbuf = refs
    n = q_ref.shape[1]
    lp = lam_ref[...]
    lam = (jnp.exp(jnp.sum(lp[0:1] * lp[1:2], axis=1, keepdims=True))
           - jnp.exp(jnp.sum(lp[2:3] * lp[3:4], axis=1, keepdims=True)) + lam_init)
    k = k_ref[0].astype(F32)
    if rope:
        k = _rope(k, cos_ref[...], sa_ref[...], sb_ref[...])
    kbuf[...] = k.astype(BF16)
    gain = g_ref[...] * (1.0 - lam_init)

    def body(i, carry):
        r0 = pl.multiple_of(i * tq, tq)
        q = q_ref[0, pl.ds(r0, tq), :].astype(F32)
        if rope:
            q = _rope(q, cos_ref[pl.ds(r0, tq), :], sa_ref[pl.ds(r0, tq), :], sb_ref[pl.ds(r0, tq), :])
        q = q * (DA_HD ** -0.5 * math.log2(math.e))
        lane = lax.broadcasted_iota(jnp.int32, q.shape, 1)
        heads = []
        for sub in range(2):
            qm = jnp.where((lane >= DA_HD) == bool(sub), q, 0.0).astype(BF16)
            sl = _dot_nt(qm, kbuf[...])
            m = jnp.max(sl, axis=-1, keepdims=True)
            if n_pre:
                sc = _dot_nt(qm, kc_ref[0])
                m = jnp.maximum(m, jnp.max(sc, axis=-1, keepdims=True))
                ec = jnp.exp2(sc - m)
            el = jnp.exp2(sl - m)
            tot = jnp.sum(el, axis=-1, keepdims=True)
            pv = _dot(el.astype(BF16), v_ref[0])
            if n_pre:
                tot = tot + jnp.sum(ec, axis=-1, keepdims=True)
                pv = pv + _dot(ec.astype(BF16), vc_ref[0])
            heads.append(pv * (1.0 / tot))
        o = heads[0] - lam * heads[1]
        o = o * lax.rsqrt(jnp.mean(o * o, axis=-1, keepdims=True) + LN_EPS) * gain
        o_ref[0, pl.ds(r0, tq), :] = o.astype(BF16)
        return carry

    lax.fori_loop(0, n // tq, body, 0)


def diff_attention(p, p_pre, tables, lam_params, subln_g, lam_init, rope):
    bsz, n, width = p.shape
    qc = (width // 2) // LANES
    nh = DA_HEADS
    n_pre = 0 if p_pre is None else p_pre.shape[1]
    tq = min(n, 256)
    col = lambda off: pl.BlockSpec((1, n, LANES), lambda b, h, off=off: (b, 0, qc + off + h))
    in_specs = [col(0), col(nh), col(2 * nh)]
    args = [p, p, p]
    if n_pre:
        pre = lambda off: pl.BlockSpec((1, n_pre, LANES), lambda b, h, off=off: (b, 0, qc + off + h))
        in_specs += [pre(nh), pre(2 * nh)]
        args += [p_pre, p_pre]
    full = lambda shape: pl.BlockSpec(shape, lambda b, h: (0,) * len(shape))
    in_specs += [full((n, LANES))] * 3 + [full((4, LANES)), full((1, LANES))]
    args += list(tables) + [lam_params, subln_g[None]]
    return pl.pallas_call(
        functools.partial(_attn_kernel, n_pre=n_pre, rope=rope, lam_init=lam_init, tq=tq),
        grid=(bsz, nh),
        in_specs=in_specs,
        out_specs=pl.BlockSpec((1, n, LANES), lambda b, h: (b, 0, h)),
        out_shape=jax.ShapeDtypeStruct((bsz, n, nh * LANES), BF16),
        scratch_shapes=[pltpu.VMEM((n, LANES), BF16)],
        compiler_params=_cparams(("arbitrary", "arbitrary")),
        name="diff_attention",
    )(*args)


def _ab_out_kernel(a_ref, b_ref, w_ref, x_ref, g_ref, lg_ref, lb_ref, o_ref):
    ka = a_ref.shape[2]
    m = _dot(a_ref[0], w_ref[0:ka, :]) + _dot(b_ref[0], w_ref[ka:, :])
    o_ref[0] = _layer_norm(DN_ALPHA * x_ref[0] + g_ref[0] * m, lg_ref[...], lb_ref[...])


def out_proj_norm(a, b, w, x, g, ln_g, ln_b):
    bsz, n, d = x.shape
    tm = min(n, 512)
    row = lambda arr: pl.BlockSpec((1, tm, arr.shape[2]), lambda bb, i: (bb, i, 0))
    vec = pl.BlockSpec((1, d), lambda bb, i: (0, 0))
    return pl.pallas_call(
        _ab_out_kernel,
        grid=(bsz, n // tm),
        in_specs=[row(a), row(b), pl.BlockSpec(w.shape, lambda bb, i: (0, 0)), row(x),
                  pl.BlockSpec((1, 1, d), lambda bb, i: (bb, 0, 0)), vec, vec],
        out_specs=row(x),
        out_shape=jax.ShapeDtypeStruct(x.shape, F32),
        compiler_params=_cparams(("arbitrary", "arbitrary")),
        name="ab_out",
    )(a, b, w, x, g, ln_g[None], ln_b[None])


def _rg_out_kernel(gate_ref, hf_ref, hb_ref, w_ref, x_ref, g_ref, lg_ref, lb_ref, o_ref):
    h = hf_ref[0].astype(F32) + hb_ref[0].astype(F32)
    m = _dot((gate_ref[0].astype(F32) * h).astype(BF16), w_ref[...])
    o_ref[0] = _layer_norm(DN_ALPHA * x_ref[0] + g_ref[0] * m, lg_ref[...], lb_ref[...])


def rg_out(gate, hf, hb, row0, w, x, g, ln_g, ln_b):
    bsz, n, d = x.shape
    tm = RG_TILE
    off = row0 // tm
    seq = pl.BlockSpec((1, tm, d), lambda bb, i: (bb, i + off, 0))
    row = pl.BlockSpec((1, tm, d), lambda bb, i: (bb, i, 0))
    vec = pl.BlockSpec((1, d), lambda bb, i: (0, 0))
    return pl.pallas_call(
        _rg_out_kernel,
        grid=(bsz, n // tm),
        in_specs=[seq, seq, seq, pl.BlockSpec(w.shape, lambda bb, i: (0, 0)), row,
                  pl.BlockSpec((1, 1, d), lambda bb, i: (bb, 0, 0)), vec, vec],
        out_specs=row,
        out_shape=jax.ShapeDtypeStruct(x.shape, F32),
        compiler_params=_cparams(("arbitrary", "arbitrary")),
        name="rg_out",
    )(gate, hf, hb, w, x, g, ln_g[None], ln_b[None])


RG_TILE = 256
RG_STEPS = 32
RG_HALO = 8


def _rg_in_kernel(xc_ref, xl_ref, scc_ref, shc_ref, scl_ref, shl_ref, w_ref, gate_ref, rec_ref, *, nct):
    d = xc_ref.shape[2]
    is_ctx = pl.program_id(1) < nct
    x = jnp.where(is_ctx, xc_ref[0], xl_ref[0])
    sc = jnp.where(is_ctx, scc_ref[0], scl_ref[0])
    sh = jnp.where(is_ctx, shc_ref[0], shl_ref[0])
    h = (x * (1.0 + sc) + sh).astype(BF16)
    y = _dot(h, w_ref[:, 0:d])
    gelu = 0.5 * y * (1.0 + jnp.tanh(math.sqrt(2.0 / math.pi) * (y + 0.044715 * (y * y * y))))
    gate_ref[0] = gelu.astype(BF16)
    rec_ref[0] = _dot(h, w_ref[:, d:])


def rg_in(xc, xl, scc, shc, scl, shl, w):
    bsz, nc, d = xc.shape
    tm = RG_TILE
    nct = nc // tm
    nt = nct + xl.shape[1] // tm
    row = pl.BlockSpec((1, tm, d), lambda b, i: (b, i, 0))
    vec = pl.BlockSpec((1, 1, d), lambda b, i: (b, 0, 0))
    return pl.pallas_call(
        functools.partial(_rg_in_kernel, nct=nct),
        grid=(bsz, nt),
        in_specs=[pl.BlockSpec((1, tm, d), lambda b, i: (b, jnp.minimum(i, nct - 1), 0)),
                  pl.BlockSpec((1, tm, d), lambda b, i: (b, jnp.maximum(i - nct, 0), 0)),
                  vec, vec, vec, vec, pl.BlockSpec(w.shape, lambda b, i: (0, 0))],
        out_specs=[row, row],
        out_shape=[jax.ShapeDtypeStruct((bsz, nt * tm, d), BF16), jax.ShapeDtypeStruct((bsz, nt * tm, d), F32)],
        compiler_params=_cparams(("arbitrary", "arbitrary")),
        name="rg_in",
    )(xc, xl, scc, shc, scl, shl, w)


def _rg_scan_kernel(cf_ref, pf_ref, nf_ref, cb_ref, pb_ref, nb_ref, cw_ref, cbias_ref, wcat_ref, bcat_ref,
                    lam_ref, hf_ref, hb_ref, u_scr, af_scr, bf_scr, ab_scr, bb_scr, hf_scr, hb_scr, carry,
                    *, nct):
    bsz, tl, d = cf_ref.shape
    bw = d // RG_BLOCKS
    i = pl.program_id(0)
    n = pl.num_programs(0)

    @pl.when(i == 0)
    def _():
        carry[...] = jnp.zeros_like(carry)

    tiles = (i, jnp.where(i < nct, nct - 1 - i, n - 1 + nct - i))
    srcs = ((cf_ref, pf_ref, nf_ref, af_scr, bf_scr), (cb_ref, pb_ref, nb_ref, ab_scr, bb_scr))
    cw = cw_ref[...]
    for dr in range(2):
        cur, prev, nxt, a_scr, b_scr = srcs[dr]
        tile = tiles[dr]
        first = (tile == 0) | (tile == nct)
        last = (tile == nct - 1) | (tile == n - 1)

        def conv_b(b, c, cur=cur, prev=prev, nxt=nxt, first=first, last=last):
            x = cur[b]
            e = jnp.concatenate([jnp.where(first, 0.0, prev[b]), x, jnp.where(last, 0.0, nxt[b])], axis=0)
            m = tl + 2 * RG_HALO
            sh = lambda k: pltpu.roll(e, k % m, 0)[RG_HALO:RG_HALO + tl]
            u_scr[b] = cw[0:1] * sh(2) + cw[1:2] * sh(1) + cw[2:3] * x + cw[3:4] * sh(-1) + cbias_ref[...]
            return c

        lax.fori_loop(0, bsz, conv_b, 0)
        u = u_scr[...].reshape(bsz * tl, d)
        ub = u.astype(BF16)
        lam = lam_ref[dr:dr + 1, :]
        log_sig = jnp.minimum(lam, 0.0) - jnp.log(1.0 + jnp.exp(-jnp.abs(lam)))
        for kb in range(RG_BLOCKS):
            sl = slice(kb * bw, (kb + 1) * bw)
            y = _dot(ub[:, sl], wcat_ref[dr, kb]) + bcat_ref[dr, kb]
            r = _sigmoid(y[:, :bw])
            g = _sigmoid(y[:, bw:])
            a = jnp.exp(RG_C * r * log_sig[:, sl])
            a_scr[:, :, sl] = a.reshape(bsz, tl, bw)
            b_scr[:, :, sl] = (jnp.sqrt(1.0 - a * a) * (g * u[:, sl])).reshape(bsz, tl, bw)

    def step(s, hs):
        hf, hb = hs
        tb = tl - 1 - s
        hf = af_scr[:, s, :] * hf + bf_scr[:, s, :]
        hf_scr[:, s, :] = hf
        hb = ab_scr[:, tb, :] * hb + bb_scr[:, tb, :]
        hb_scr[:, tb, :] = hb
        return hf, hb

    hf, hb = lax.fori_loop(0, tl, step, (carry[0], carry[1]), unroll=2)
    carry[0] = hf
    carry[1] = hb
    hf_ref[...] = hf_scr[...].astype(BF16)
    hb_ref[...] = hb_scr[...].astype(BF16)


def rg_scan(rec, n_ctx, conv_w, conv_b, wcat, bcat, lam):
    bsz, n, d = rec.shape
    tl = RG_STEPS
    nt = n // tl
    nct = n_ctx // tl
    per = tl // RG_HALO
    fmap = lambda i: i
    bmap = lambda i: jnp.where(i < nct, nct - 1 - i, nt - 1 + nct - i)

    def views(tmap):
        return [pl.BlockSpec((bsz, tl, d), lambda i: (0, tmap(i), 0)),
                pl.BlockSpec((bsz, RG_HALO, d), lambda i: (0, jnp.maximum(tmap(i) * per - 1, 0), 0)),
                pl.BlockSpec((bsz, RG_HALO, d), lambda i: (0, jnp.minimum((tmap(i) + 1) * per, n // RG_HALO - 1), 0))]

    full = lambda arr: pl.BlockSpec(arr.shape, lambda i: (0,) * arr.ndim)
    cb = conv_b[None]
    slab = pltpu.VMEM((bsz, tl, d), F32)
    return pl.pallas_call(
        functools.partial(_rg_scan_kernel, nct=nct),
        grid=(nt,),
        in_specs=views(fmap) + views(bmap) + [full(conv_w), full(cb), full(wcat), full(bcat), full(lam)],
        out_specs=[pl.BlockSpec((bsz, tl, d), lambda i: (0, fmap(i), 0)),
                   pl.BlockSpec((bsz, tl, d), lambda i: (0, bmap(i), 0))],
        out_shape=[jax.ShapeDtypeStruct((bsz, n, d), BF16)] * 2,
        scratch_shapes=[slab] * 7 + [pltpu.VMEM((2, bsz, d), F32)],
        compiler_params=_cparams(("arbitrary",)),
        name="rg_scan",
    )(rec, rec, rec, rec, rec, rec, conv_w, cb, wcat, bcat, lam)


def _route_kernel(x_ref, sc_ref, sh_ref, wh_ref, wl_ref, bias_ref, gates_ref, rank_ref, rank_t_ref):
    t = x_ref[0] * (1.0 + sc_ref[0]) + sh_ref[0]
    th = t.astype(BF16)
    tl = (t - th.astype(F32)).astype(BF16)
    logits = _dot_nt(wh_ref[...], th) + (_dot_nt(wh_ref[...], tl) + _dot_nt(wl_ref[...], th))
    scores = jax.nn.sigmoid(logits)
    sel = scores + bias_ref[...]
    ninf = -jnp.inf
    sub = lax.broadcasted_iota(jnp.int32, (EPG, sel.shape[1]), 0)
    gscore = []
    for g in range(N_GROUPS):
        slab = sel[g * EPG:(g + 1) * EPG]
        m1 = jnp.max(slab, axis=0, keepdims=True)
        first = jnp.min(jnp.where(slab == m1, sub, EPG), axis=0, keepdims=True)
        m2 = jnp.max(jnp.where(sub == first, ninf, slab), axis=0, keepdims=True)
        gscore.append(m1 + m2)
    slabs = []
    for g in range(N_GROUPS):
        beaten = jnp.zeros(gscore[g].shape, F32)
        for o in range(N_GROUPS):
            if o != g:
                wins = (gscore[o] >= gscore[g]) if o < g else (gscore[o] > gscore[g])
                beaten = beaten + jnp.where(wins, 1.0, 0.0)
        keep = jnp.broadcast_to(beaten < TOPK_GROUPS, (EPG, sel.shape[1]))
        slabs.append(jnp.where(keep, sel[g * EPG:(g + 1) * EPG], ninf))
    v = jnp.concatenate(slabs, axis=0)
    eidx = lax.broadcasted_iota(jnp.int32, v.shape, 0)
    w = jnp.zeros(v.shape, F32)
    chosen = jnp.zeros(v.shape, F32)
    for _ in range(TOP_K):
        m = jnp.max(v, axis=0, keepdims=True)
        first = jnp.min(jnp.where(v == m, eidx, N_EXPERTS), axis=0, keepdims=True)
        hit = eidx == first
        w = jnp.where(hit, scores, w)
        chosen = jnp.where(hit, 1.0, chosen)
        v = jnp.where(hit, ninf, v)
    gates = w / jnp.sum(w, axis=0, keepdims=True) * ROUTED_SCALE
    tm = v.shape[1]
    upper = jnp.where(lax.broadcasted_iota(jnp.int32, (tm, tm), 0) < lax.broadcasted_iota(jnp.int32, (tm, tm), 1),
                      1.0, 0.0).astype(BF16)
    rank = jnp.where(chosen > 0.0, _dot(chosen.astype(BF16), upper), -1.0)
    pad = LANES - N_EXPERTS
    gates_ref[...] = jnp.concatenate([gates, jnp.zeros((pad, tm), F32)], axis=0).T
    rank_p = jnp.concatenate([rank, jnp.full((pad, tm), -1.0, F32)], axis=0)
    rank_ref[...] = rank_p.T.astype(BF16)
    rank_t_ref[...] = rank_p.astype(BF16)


MOE_WIN = 256
SEG_ALIGN = 16
MOE_TILE = 512
WIN_ROWS = -(-(MOE_WIN * TOP_K + N_EXPERTS * (SEG_ALIGN - 1)) // 256) * 256
SEG_SIZES = tuple(SEG_ALIGN << k for k in reversed(range((MOE_WIN // SEG_ALIGN).bit_length())))
WIN_GRANULES = WIN_ROWS // SEG_ALIGN
WAIT_SIZES = tuple(SEG_ALIGN << k for k in reversed(range((WIN_ROWS // SEG_ALIGN).bit_length())))


def route(x, sc, sh, rw_hi, rw_lo, bias):
    bsz, n, d = x.shape
    tm = MOE_WIN
    nt = n // tm
    full = lambda arr: pl.BlockSpec(arr.shape, lambda b, i: (0,) * arr.ndim)
    vec = pl.BlockSpec((1, 1, d), lambda b, i: (b, 0, 0))
    tok = pl.BlockSpec((tm, LANES), lambda b, i: (b * nt + i, 0))
    return pl.pallas_call(
        _route_kernel,
        grid=(bsz, nt),
        in_specs=[pl.BlockSpec((1, tm, d), lambda b, i: (b, i, 0)), vec, vec, full(rw_hi), full(rw_lo),
                  full(bias)],
        out_specs=[tok, tok, pl.BlockSpec((LANES, tm), lambda b, i: (0, b * nt + i))],
        out_shape=[jax.ShapeDtypeStruct((bsz * n, LANES), F32), jax.ShapeDtypeStruct((bsz * n, LANES), BF16),
                   jax.ShapeDtypeStruct((LANES, bsz * n), BF16)],
        compiler_params=_cparams(("arbitrary", "arbitrary")),
        name="route",
    )(x, sc, sh, rw_hi, rw_lo, bias)


def moe_plan(rank_tm, nw):
    cnt = (rank_tm[:, :N_EXPERTS] >= 0).reshape(nw, MOE_WIN, N_EXPERTS).sum(axis=1).astype(jnp.int32)
    pc = (cnt + SEG_ALIGN - 1) // SEG_ALIGN * SEG_ALIGN
    lo = jnp.cumsum(pc, axis=1) - pc
    used = pc.sum(axis=0)
    tot = (used + MOE_TILE - 1) // MOE_TILE * MOE_TILE
    ends = jnp.cumsum(tot)
    go = (ends - tot)[None, :] + jnp.cumsum(pc, axis=0) - pc
    n_tiles = (nw * MOE_WIN * TOP_K + nw * N_EXPERTS * (SEG_ALIGN - 1)) // MOE_TILE + N_EXPERTS
    tile_expert = jnp.sum(jnp.arange(n_tiles, dtype=jnp.int32)[:, None] >= (ends // MOE_TILE)[None, :], axis=1)
    col = lambda a: jnp.pad(a.astype(F32), ((0, 0), (0, LANES - N_EXPERTS)))
    grow = jnp.arange(WIN_GRANULES, dtype=jnp.int32)[None, :] * SEG_ALIGN
    owner = jnp.minimum(jnp.sum(grow[:, :, None] >= (lo + pc)[:, None, :], axis=-1), N_EXPERTS - 1)
    gdst = jnp.take_along_axis(go - lo, owner, axis=1) + grow
    return dict(
        gdst=gdst.reshape(-1).astype(jnp.int32),
        tail_pc=tot - used, tail_go=ends - tot + used, used=pc.sum(axis=1),
        lo_row=col(lo)[:, None, :], pc_row=col(pc)[:, None, :],
        lo_col=col(lo)[:, :, None], pc_col=col(pc)[:, :, None],
        tile_expert=jnp.minimum(tile_expert, N_EXPERTS - 1).astype(jnp.int32),
        n_used=(ends[-1] // MOE_TILE).astype(jnp.int32).reshape(1), n_tiles=n_tiles)


def _segment_copies(lo_s, pc_s, go_s, w, local_ref, sorted_hbm, sem, to_sorted, wait):
    def body(e, carry):
        idx = w * N_EXPERTS + e
        lo = 0 if lo_s is None else lo_s[idx]
        pc = pc_s[idx]
        go = go_s[idx]
        off = jnp.int32(0)
        for size in SEG_SIZES:
            bit = pc & size

            @pl.when(bit != 0)
            def _(off=off, size=size):
                loc = local_ref.at[pl.ds(pl.multiple_of(lo + off, SEG_ALIGN), size), :]
                glo = sorted_hbm.at[pl.ds(pl.multiple_of(go + off, SEG_ALIGN), size), :]
                cp = pltpu.make_async_copy(loc, glo, sem) if to_sorted else pltpu.make_async_copy(glo, loc, sem)
                if wait:
                    cp.wait()
                else:
                    cp.start()

            off = off + bit
        return carry

    lax.fori_loop(0, N_EXPERTS, body, 0)


def _granule_copies(gdst_s, w, rows, local_ref, sorted_hbm, sem, to_sorted):
    def body(g, carry):
        loc = local_ref.at[pl.ds(pl.multiple_of(g * SEG_ALIGN, SEG_ALIGN), SEG_ALIGN), :]
        dst = pl.multiple_of(gdst_s[w * WIN_GRANULES + g], SEG_ALIGN)
        glo = sorted_hbm.at[pl.ds(dst, SEG_ALIGN), :]
        cp = pltpu.make_async_copy(loc, glo, sem) if to_sorted else pltpu.make_async_copy(glo, loc, sem)
        cp.start()
        return carry

    lax.fori_loop(0, rows // SEG_ALIGN, body, 0)


def _wait_rows(rows, local_ref, sorted_hbm, sem, to_sorted):
    for size in WAIT_SIZES:
        @pl.when((rows & size) != 0)
        def _(size=size):
            loc = local_ref.at[pl.ds(0, size), :]
            glo = sorted_hbm.at[pl.ds(0, size), :]
            cp = pltpu.make_async_copy(loc, glo, sem) if to_sorted else pltpu.make_async_copy(glo, loc, sem)
            cp.wait()


def _dispatch_kernel(gdst_s, tpc_s, tgo_s, used_s, x_ref, sc_ref, sh_ref, rank_t_ref, lo_ref,
                     pc_ref, sorted_hbm, rows_ref, sem):
    w = pl.program_id(0) * pl.num_programs(1) + pl.program_id(1)
    used = used_s[w]

    @pl.when(w == 0)
    def _():
        rows_ref[0:MOE_TILE, :] = jnp.zeros((MOE_TILE, rows_ref.shape[1]), BF16)
        _segment_copies(None, tpc_s, tgo_s, 0, rows_ref, sorted_hbm, sem.at[0], True, False)
        _segment_copies(None, tpc_s, tgo_s, 0, rows_ref, sorted_hbm, sem.at[0], True, True)

    t = (x_ref[0] * (1.0 + sc_ref[0]) + sh_ref[0]).astype(BF16)
    lo = lo_ref[0]
    hi = lo + pc_ref[0]
    chunk = 256
    for c in range(WIN_ROWS // chunk):
        @pl.when(c * chunk < used)
        def _(c=c):
            r = (lax.broadcasted_iota(jnp.int32, (chunk, LANES), 0) + c * chunk).astype(F32)
            owner = (r >= lo) & (r < hi)
            seg_lo = jnp.sum(jnp.where(owner, lo, 0.0), axis=1, keepdims=True)
            rank_of = _dot(jnp.where(owner, 1.0, 0.0).astype(BF16), rank_t_ref[...])
            pick = jnp.where(rank_of == r[:, 0:1] - seg_lo, 1.0, 0.0).astype(BF16)
            rows_ref[c * chunk:(c + 1) * chunk, :] = _dot(pick, t).astype(BF16)

    _granule_copies(gdst_s, w, used, rows_ref, sorted_hbm, sem.at[0], True)
    _wait_rows(used, rows_ref, sorted_hbm, sem.at[0], True)


def moe_dispatch(x, sc, sh, rank_t, plan, n_rows):
    bsz, n, d = x.shape
    nwb = n // MOE_WIN
    vec = pl.BlockSpec((1, 1, d), lambda b, i, *_: (b, 0, 0))
    meta = pl.BlockSpec((1, 1, LANES), lambda b, i, *_: (b * nwb + i, 0, 0))
    return pl.pallas_call(
        _dispatch_kernel,
        grid_spec=pltpu.PrefetchScalarGridSpec(
            num_scalar_prefetch=4,
            grid=(bsz, nwb),
            in_specs=[pl.BlockSpec((1, MOE_WIN, d), lambda b, i, *_: (b, i, 0)), vec, vec,
                      pl.BlockSpec((LANES, MOE_WIN), lambda b, i, *_: (0, b * nwb + i)), meta, meta],
            out_specs=pl.BlockSpec(memory_space=pl.ANY),
            scratch_shapes=[pltpu.VMEM((WIN_ROWS, d), BF16), pltpu.SemaphoreType.DMA((1,))]),
        out_shape=jax.ShapeDtypeStruct((n_rows, d), BF16),
        compiler_params=_cparams(("arbitrary", "arbitrary")),
        name="moe_dispatch",
    )(plan["gdst"], plan["tail_pc"], plan["tail_go"], plan["used"], x, sc, sh, rank_t,
      plan["lo_row"], plan["pc_row"])


def _expert_kernel(te_s, nu_s, x_ref, wg_ref, wu_ref, wd_ref, o_ref):
    @pl.when(pl.program_id(0) < nu_s[0])
    def _():
        x = x_ref[...]
        h = _silu(_dot(x, wg_ref[0, 0].astype(BF16))) * _dot(x, wu_ref[0, 0].astype(BF16))
        o_ref[...] = _dot(h.astype(BF16), wd_ref[0, 0].astype(BF16)).astype(BF16)


def moe_experts(xs, plan, layer, wg, wu, wd):
    n_rows, d = xs.shape
    de = wg.shape[3]
    tile = lambda i, te, nu: jnp.minimum(i, nu[0] - 1)
    row = pl.BlockSpec((MOE_TILE, d), lambda i, te, nu: (tile(i, te, nu), 0))
    return pl.pallas_call(
        _expert_kernel,
        grid_spec=pltpu.PrefetchScalarGridSpec(
            num_scalar_prefetch=2,
            grid=(plan["n_tiles"],),
            in_specs=[row,
                      pl.BlockSpec((1, 1, d, de), lambda i, te, nu: (layer, te[tile(i, te, nu)], 0, 0)),
                      pl.BlockSpec((1, 1, d, de), lambda i, te, nu: (layer, te[tile(i, te, nu)], 0, 0)),
                      pl.BlockSpec((1, 1, de, d), lambda i, te, nu: (layer, te[tile(i, te, nu)], 0, 0))],
            out_specs=row),
        out_shape=jax.ShapeDtypeStruct((n_rows, d), BF16),
        compiler_params=_cparams(("arbitrary",)),
        name="moe_experts",
    )(plan["tile_expert"], plan["n_used"], xs, wg, wu, wd)


def _combine_kernel(gdst_s, used_s, x_ref, sc_ref, sh_ref, gf_ref, gates_ref, rank_ref, lo_ref, pc_ref,
                    sorted_hbm, sg_ref, su_ref, sd_ref, lg_ref, lb_ref, o_ref, rows_ref, wts_ref, acc_ref, sem):
    first = (pl.program_id(0) == 0) & (pl.program_id(1) == 0)

    @pl.when(first)
    def _():
        rows_ref[...] = jnp.zeros_like(rows_ref)

    w = pl.program_id(0) * pl.num_programs(1) + pl.program_id(1)
    used = used_s[w]
    _granule_copies(gdst_s, w, used, rows_ref, sorted_hbm, sem.at[0], False)
    x = x_ref[0]
    t = (x * (1.0 + sc_ref[0]) + sh_ref[0]).astype(BF16)
    acc_ref[...] = _dot((_silu(_dot(t, sg_ref[0].astype(BF16))) * _dot(t, su_ref[0].astype(BF16))).astype(BF16),
                        sd_ref[0].astype(BF16))
    lo = lo_ref[0]
    hi = lo + pc_ref[0]
    rank = rank_ref[...]
    gates = gates_ref[...].astype(BF16)
    chunk = 512
    for c in range(WIN_ROWS // chunk):
        @pl.when(c * chunk < used)
        def _(c=c):
            r = (lax.broadcasted_iota(jnp.int32, (LANES, chunk), 1) + c * chunk).astype(F32)
            owner = (r >= lo) & (r < hi)
            seg_lo = jnp.sum(jnp.where(owner, lo, 0.0), axis=0, keepdims=True)
            owner_b = jnp.where(owner, 1.0, 0.0).astype(BF16)
            rank_of = _dot(rank, owner_b)
            gate_of = _dot(gates, owner_b)
            wts_ref[:, c * chunk:(c + 1) * chunk] = jnp.where(rank_of == r[0:1, :] - seg_lo, gate_of,
                                                              0.0).astype(BF16)

    _wait_rows(used, rows_ref, sorted_hbm, sem.at[0], False)
    for c in range(WIN_ROWS // chunk):
        @pl.when(c * chunk < used)
        def _(c=c):
            acc_ref[...] += _dot(wts_ref[:, c * chunk:(c + 1) * chunk], rows_ref[c * chunk:(c + 1) * chunk, :])

    o_ref[0] = _layer_norm(DN_ALPHA * x + gf_ref[0] * acc_ref[...], lg_ref[...], lb_ref[...])


def moe_combine(x, sc, sh, gf, gates, rank_tm, plan, ys, layer, sg, su, sd, ln_g, ln_b):
    bsz, n, d = x.shape
    nwb = n // MOE_WIN
    row = pl.BlockSpec((1, MOE_WIN, d), lambda b, i, *_: (b, i, 0))
    vec = pl.BlockSpec((1, 1, d), lambda b, i, *_: (b, 0, 0))
    tok = pl.BlockSpec((MOE_WIN, LANES), lambda b, i, *_: (b * nwb + i, 0))
    meta = pl.BlockSpec((1, LANES, 1), lambda b, i, *_: (b * nwb + i, 0, 0))
    lnv = pl.BlockSpec((1, d), lambda b, i, *_: (0, 0))
    return pl.pallas_call(
        _combine_kernel,
        grid_spec=pltpu.PrefetchScalarGridSpec(
            num_scalar_prefetch=2,
            grid=(bsz, nwb),
            in_specs=[row, vec, vec, vec, tok, tok, meta, meta,
                      pl.BlockSpec(memory_space=pl.ANY),
                      pl.BlockSpec((1, d, sg.shape[2]), lambda b, i, *_: (layer, 0, 0)),
                      pl.BlockSpec((1, d, su.shape[2]), lambda b, i, *_: (layer, 0, 0)),
                      pl.BlockSpec((1, sd.shape[1], d), lambda b, i, *_: (layer, 0, 0)),
                      lnv, lnv],
            out_specs=row,
            scratch_shapes=[pltpu.VMEM((WIN_ROWS, d), BF16), pltpu.VMEM((MOE_WIN, WIN_ROWS), BF16),
                            pltpu.VMEM((MOE_WIN, d), F32), pltpu.SemaphoreType.DMA((1,))]),
        out_shape=jax.ShapeDtypeStruct(x.shape, F32),
        compiler_params=_cparams(("arbitrary", "arbitrary")),
        name="moe_combine",
    )(plan["gdst"], plan["used"], x, sc, sh, gf, gates, rank_tm, plan["lo_col"],
      plan["pc_col"], ys,
      sg, su, sd, ln_g[None], ln_b[None])


def kernel(x, c, ctx, c_ctx, w_mod, b_mod, ln1_g, ln1_b, ln2_g, ln2_b, ab_w_in, ab_w_out, hy_short_w, hy_short_b, hy_fw1, hy_fb1, hy_fw2, hy_fb2, hy_fw3, hy_fb3, hy_fw4, hy_freq, hy_decay, hy_skip, da_lq1, da_lk1, da_lq2, da_lk2, da_subln_g, rg_w_in, rg_w_out, rg_conv_w, rg_conv_b, rg_wa, rg_ba, rg_wx, rg_bx, rg_lambda, router_w, router_bias, ex_w_gate, ex_w_up, ex_w_down, sh_w_gate, sh_w_up, sh_w_down):
    return forward(dict(locals()))


def split_mod(mod_l, bsz, d):
    ml = [mod_l[:bsz, k * d:(k + 1) * d][:, None, :] for k in range(N_MOD)]
    mc = [jnp.broadcast_to(mod_l[bsz, k * d:(k + 1) * d][None, None, :], (bsz, 1, d)) for k in range(N_MOD)]
    return ml, mc


def mixer_even(p, l, xs, cs, ml, mc, need_ctx):
    e = l // 2
    n_lat, n_ctx = xs.shape[1], cs.shape[1]
    lam_init = 0.8 - 0.6 * math.exp(-0.3 * l)
    w_in = p["ab_w_in"][e].astype(BF16)
    w_out = p["ab_w_out"][e].astype(BF16)
    p_lat = ab_in(xs, ml[1], ml[0], w_in)
    p_ctx = ab_in(cs, mc[1], mc[0], w_in)
    filt_p = tuple(p[k][e] for k in ("hy_fw1", "hy_fb1", "hy_fw2", "hy_fb2", "hy_fw3", "hy_fb3", "hy_fw4",
                                     "hy_freq", "hy_decay"))
    lam_params = jnp.pad(jnp.stack([p["da_lq1"][e], p["da_lk1"][e], p["da_lq2"][e], p["da_lk2"][e]]),
                         ((0, 0), (0, LANES - DA_HD)))
    rope_lat = rope_tables(n_lat)
    rope_ctx = tuple(t[:n_ctx] for t in rope_lat)

    def hyena(proj, n):
        dft = dft_tables(n)
        hs, hd = hyena_filter_parts(n, *filt_p)
        tabs = filter_spectrum(hs, hd, dft[0], dft[1])
        return hyena_mixer(proj, p["hy_short_w"][e], p["hy_short_b"][e], p["hy_skip"][e], tabs, *dft)

    g1, b1 = p["ln1_g"][l], p["ln1_b"][l]
    sub_g = p["da_subln_g"][e]
    y_lat = hyena(p_lat, n_lat)
    o_lat = diff_attention(p_lat, p_ctx, rope_lat, lam_params, sub_g, lam_init, True)
    xs_new = out_proj_norm(y_lat, o_lat, w_out, xs, ml[2], g1, b1)
    if need_ctx:
        y_ctx = hyena(p_ctx, n_ctx)
        o_ctx = diff_attention(p_ctx, None, rope_ctx, lam_params, sub_g, lam_init, False)
        cs = out_proj_norm(y_ctx, o_ctx, w_out, cs, mc[2], g1, b1)
    return xs_new, cs


def mixer_odd(p, l, xs, cs, ml, mc, need_ctx):
    o = l // 2
    w_in = p["rg_w_in"][o].astype(BF16)
    w_out = p["rg_w_out"][o].astype(BF16)
    n_ctx = cs.shape[1]
    gate, rec = rg_in(cs, xs, mc[1], mc[0], ml[1], ml[0], w_in)
    wcat = jnp.concatenate([p["rg_wa"][o], p["rg_wx"][o]], axis=-1).astype(BF16)
    split = lambda v: v.reshape(2, RG_BLOCKS, 1, -1)
    bcat = jnp.concatenate([split(p["rg_ba"][o]), split(p["rg_bx"][o])], axis=-1)
    hf, hb = rg_scan(rec, n_ctx, p["rg_conv_w"][o], p["rg_conv_b"][o], wcat, bcat, p["rg_lambda"][o])
    g1, b1 = p["ln1_g"][l], p["ln1_b"][l]
    xs = rg_out(gate, hf, hb, n_ctx, w_out, xs, ml[2], g1, b1)
    if need_ctx:
        cs = rg_out(gate, hf, hb, 0, w_out, cs, mc[2], g1, b1)
    return xs, cs


def moe_block(p, l, h, m):
    rw_t = p["router_w"][l].T
    rw_hi = rw_t.astype(BF16)
    rw_lo = (rw_t - rw_hi.astype(F32)).astype(BF16)
    gates, rank_tm, rank_t = route(h, m[4], m[3], rw_hi, rw_lo, p["router_bias"][l][:, None])
    plan = moe_plan(rank_tm, gates.shape[0] // MOE_WIN)
    xs = moe_dispatch(h, m[4], m[3], rank_t, plan, plan["n_tiles"] * MOE_TILE)
    ys = moe_experts(xs, plan, l, p["ex_w_gate"], p["ex_w_up"], p["ex_w_down"])
    return moe_combine(h, m[4], m[3], m[5], gates, rank_tm, plan, ys, l, p["sh_w_gate"], p["sh_w_up"],
                       p["sh_w_down"], p["ln2_g"][l], p["ln2_b"][l])


def run_layer(p, l, xs, cs, mod_l):
    bsz, _, d = xs.shape
    need_ctx = l < DEPTH - 1
    ml, mc = split_mod(mod_l, bsz, d)
    mixer = mixer_even if l % 2 == 0 else mixer_odd
    xs, cs = mixer(p, l, xs, cs, ml, mc, need_ctx)
    xs = moe_block(p, l, xs, ml)
    if need_ctx:
        cs = moe_block(p, l, cs, mc)
    return xs, cs


def conditioning_rows(c, c_ctx):
    bsz, d = c.shape
    rows = 8 * ((bsz + 1 + 7) // 8)
    return jnp.zeros((rows, d), F32).at[:bsz].set(c).at[bsz].set(c_ctx)


def forward(p):
    mod = modulation(conditioning_rows(p["c"], p["c_ctx"]), p["w_mod"], p["b_mod"])
    xs, cs = p["x"], p["ctx"]
    for l in range(DEPTH):
        xs, cs = run_layer(p, l, xs, cs, mod[l])
    return xs
```

```python
import functools
import math

import jax
import jax.numpy as jnp
from jax import lax
from jax.experimental import pallas as pl
from jax.experimental.pallas import tpu as pltpu

F32 = jnp.float32
BF16 = jnp.bfloat16

DEPTH = 4
GRID_W = 64
N_MOD = 6
LN_EPS = 1e-5
DN_ALPHA = (2.0 * DEPTH) ** 0.25

HY_ORDER = 2
HY_EMB = 33
HY_BANDS = (HY_EMB - 1) // 2
DA_HEADS = 4
DA_HD = 64
DA_VD = 2 * DA_HD
ROPE_BASE = 10000.0
ROPE_NF = DA_HD // 4
RG_BLOCKS = 8
RG_C = 8.0
N_EXPERTS = 64
N_GROUPS = 8
EPG = N_EXPERTS // N_GROUPS
TOPK_GROUPS = 4
TOP_K = 8
ROUTED_SCALE = 2.5

LANES = 128
VMEM_LIMIT = 56 * 1024 * 1024


def _cparams(sem):
    return pltpu.CompilerParams(dimension_semantics=sem, vmem_limit_bytes=VMEM_LIMIT)


def _dot(a, b):
    return jnp.dot(a, b, preferred_element_type=F32)


def _dot_nt(a, b):
    return lax.dot_general(a, b, (((1,), (1,)), ((), ())), preferred_element_type=F32)


def _silu(x):
    return x * jax.nn.sigmoid(x)


def _sigmoid(x):
    return 0.5 * jnp.tanh(0.5 * x) + 0.5


def _layer_norm(r, g, b):
    mu = jnp.mean(r, axis=-1, keepdims=True)
    d = r - mu
    var = jnp.mean(d * d, axis=-1, keepdims=True)
    return d * lax.rsqrt(var + LN_EPS) * g + b


def _mod_kernel(cc_ref, w_ref, b_ref, o_ref):
    s = _silu(cc_ref[...])
    o_ref[0] = _dot(s.astype(BF16), w_ref[0].astype(BF16)) + b_ref[0]


def modulation(cc, w_mod, b_mod):
    r, d = cc.shape
    n = w_mod.shape[-1]
    tn = 1536
    return pl.pallas_call(
        _mod_kernel,
        grid=(DEPTH, n // tn),
        in_specs=[pl.BlockSpec((r, d), lambda l, j: (0, 0)),
                  pl.BlockSpec((1, d, tn), lambda l, j: (l, 0, j)),
                  pl.BlockSpec((1, 1, tn), lambda l, j: (l, 0, j))],
        out_specs=pl.BlockSpec((1, r, tn), lambda l, j: (l, 0, j)),
        out_shape=jax.ShapeDtypeStruct((DEPTH, r, n), F32),
        compiler_params=_cparams(("arbitrary", "arbitrary")),
        name="modulation",
    )(cc, w_mod, b_mod.reshape(DEPTH, 1, n))


def _ab_in_kernel(x_ref, sc_ref, sh_ref, w_ref, o_ref, *, tn):
    h = (x_ref[0] * (1.0 + sc_ref[0]) + sh_ref[0]).astype(BF16)
    for j in range(w_ref.shape[1] // tn):
        o_ref[0, :, j * tn:(j + 1) * tn] = _dot(h, w_ref[:, j * tn:(j + 1) * tn]).astype(BF16)


def ab_in(x, sc, sh, w):
    bsz, n, d = x.shape
    nout = w.shape[1]
    tm = min(n, 512)
    return pl.pallas_call(
        functools.partial(_ab_in_kernel, tn=1024),
        grid=(bsz, n // tm),
        in_specs=[pl.BlockSpec((1, tm, d), lambda b, i: (b, i, 0)),
                  pl.BlockSpec((1, 1, d), lambda b, i: (b, 0, 0)),
                  pl.BlockSpec((1, 1, d), lambda b, i: (b, 0, 0)),
                  pl.BlockSpec((d, nout), lambda b, i: (0, 0))],
        out_specs=pl.BlockSpec((1, tm, nout), lambda b, i: (b, i, 0)),
        out_shape=jax.ShapeDtypeStruct((bsz, n, nout), BF16),
        compiler_params=_cparams(("arbitrary", "arbitrary")),
        name="ab_in",
    )(x, sc, sh, w)


def _filter_kernel(z_ref, w1_ref, b1_ref, w2_ref, b2_ref, w3_ref, b3_ref, w4_ref, fr_ref, dec_ref,
                   hs_ref, hd_ref):
    hp = lax.Precision.HIGHEST
    z = z_ref[...]
    fr = fr_ref[...]
    h = jnp.sin(fr * (jnp.dot(z, w1_ref[...], precision=hp, preferred_element_type=F32) + b1_ref[...]))
    h = jnp.sin(fr * (jnp.dot(h, w2_ref[...], precision=hp, preferred_element_type=F32) + b2_ref[...]))
    h = jnp.sin(fr * (jnp.dot(h, w3_ref[...], precision=hp, preferred_element_type=F32) + b3_ref[...]))
    t = z[:, 0:1]
    f = jnp.dot(h, w4_ref[...], precision=hp, preferred_element_type=F32) * jnp.exp(-t * jnp.abs(dec_ref[...]))
    w = f.shape[1] // (2 * HY_ORDER)
    for o in range(HY_ORDER):
        hf = f[:, (2 * o) * w:(2 * o + 1) * w]
        hb = f[:, (2 * o + 1) * w:(2 * o + 2) * w]
        hs_ref[o] = (hf + hb).astype(BF16)
        hd_ref[o] = (hb - hf).astype(BF16)


def hyena_filter_parts(n, w1, b1, w2, b2, w3, b3, w4, freq, decay):
    t = jnp.linspace(0.0, 1.0, n, dtype=F32)[:, None]
    f = jnp.linspace(1e-4, HY_BANDS - 1, HY_BANDS, dtype=F32)
    ang = (2.0 * math.pi / n) * jnp.arange(n, dtype=F32)[:, None] * f[None, :]
    z = jnp.concatenate([t, jnp.cos(ang), -jnp.sin(ang)], axis=-1)
    z = jnp.pad(z, ((0, 0), (0, LANES - HY_EMB)))
    w1p = jnp.pad(w1, ((0, LANES - HY_EMB), (0, 0)))
    hid = w1.shape[1]
    nf = w4.shape[1]
    w = nf // (2 * HY_ORDER)
    tm = min(n, 512)
    full = lambda shape: pl.BlockSpec(shape, lambda i: (0,) * len(shape))
    return pl.pallas_call(
        _filter_kernel,
        grid=(n // tm,),
        in_specs=[pl.BlockSpec((tm, LANES), lambda i: (i, 0)),
                  full((LANES, hid)), full((1, hid)), full((hid, hid)), full((1, hid)),
                  full((hid, hid)), full((1, hid)), full((hid, nf)), full((1, hid)), full((1, nf))],
        out_specs=[pl.BlockSpec((HY_ORDER, tm, w), lambda i: (0, i, 0)),
                   pl.BlockSpec((HY_ORDER, tm, w), lambda i: (0, i, 0))],
        out_shape=[jax.ShapeDtypeStruct((HY_ORDER, n, w), BF16)] * 2,
        compiler_params=_cparams(("arbitrary",)),
        name="hyena_filter",
    )(z, w1p, b1[None], w2, b2[None], w3, b3[None], w4, freq[None], decay[None])


def dft_tables(n):
    k = (jnp.arange(n, dtype=jnp.int32)[:, None] * jnp.arange(n, dtype=jnp.int32)[None, :]) % (2 * n)
    ang = k.astype(F32) * (math.pi / n)
    c = jnp.cos(ang)
    s = jnp.sin(ang)
    alt = jnp.where(jnp.arange(n) % 2 == 0, 1.0, -1.0).astype(F32)
    s_fwd = s.at[0, :].set(alt)
    s_inv = s.at[:, 0].set(alt)
    return c.astype(BF16), s_fwd.astype(BF16), s_inv.astype(BF16)


def _spectrum_kernel(hs_ref, hd_ref, c_ref, s_ref, o_ref, *, n):
    f0 = pl.program_id(1) * c_ref.shape[0]
    hs = hs_ref[0]
    hr = _dot(c_ref[...], hs)
    hq = _dot(s_ref[...], hs)
    hi = _dot(s_ref[...], hd_ref[0])
    row = lax.broadcasted_iota(jnp.int32, hr.shape, 0) + f0
    dc = row == 0
    inv = 1.0 / (2 * n)
    o_ref[0, 0] = jnp.where(dc, hr * inv, hr * (2.0 * inv))
    o_ref[0, 1] = jnp.where(dc, 0.0, hi * (2.0 * inv))
    o_ref[0, 2] = jnp.where(dc, hq * inv, hr * (2.0 * inv))


def filter_spectrum(hs, hd, c, s_fwd):
    _, n, w = hs.shape
    tf = min(n, 256)
    return pl.pallas_call(
        functools.partial(_spectrum_kernel, n=n),
        grid=(HY_ORDER, n // tf),
        in_specs=[pl.BlockSpec((1, n, w), lambda o, f: (o, 0, 0)),
                  pl.BlockSpec((1, n, w), lambda o, f: (o, 0, 0)),
                  pl.BlockSpec((tf, n), lambda o, f: (f, 0)),
                  pl.BlockSpec((tf, n), lambda o, f: (f, 0))],
        out_specs=pl.BlockSpec((1, 3, tf, w), lambda o, f: (o, 0, f, 0)),
        out_shape=jax.ShapeDtypeStruct((HY_ORDER, 3, n, w), F32),
        compiler_params=_cparams(("arbitrary", "arbitrary")),
        name="filter_spectrum",
    )(hs, hd, c, s_fwd)


def _shift_rows(u, k):
    n = u.shape[0]
    row = lax.broadcasted_iota(jnp.int32, u.shape, 0)
    r = pltpu.roll(u, k % n, 0)
    if k > 0:
        return jnp.where(row < k, 0.0, r)
    return jnp.where(row >= n + k, 0.0, r)


def _hyena_kernel(p_ref, sw_ref, sb_ref, skip_ref, c_ref, s_ref, ct_ref, st_ref, tab_ref, o_ref,
                  sig_ref, z_ref, acc_ref, *, w):
    ph = pl.program_id(1)
    f = pl.program_id(2)
    last = pl.num_programs(2) - 1

    def short_conv(lo):
        u = p_ref[0, :, lo:lo + w].astype(F32)
        cw = sw_ref[:, lo:lo + w]
        return (cw[0:1] * _shift_rows(u, 1) + cw[1:2] * u + cw[2:3] * _shift_rows(u, -1)
                + sb_ref[:, lo:lo + w])

    @pl.when((ph == 0) & (f == 0))
    def _():
        sig_ref[...] = short_conv(0).astype(BF16)

    @pl.when(f == 0)
    def _():
        acc_ref[...] = jnp.zeros_like(acc_ref)

    sig = sig_ref[...]
    p = _dot(c_ref[...], sig)
    q = _dot(s_ref[...], sig)
    hr = tab_ref[0, 0]
    hi = tab_ref[0, 1]
    hq = tab_ref[0, 2]
    zr = (p * hr + q * hi).astype(BF16)
    zq = (q * hq - p * hi).astype(BF16)
    acc_ref[...] += _dot(ct_ref[...], zr) + _dot(st_ref[...], zq)

    @pl.when((f == last) & (ph == 0))
    def _():
        z = short_conv(w) * (acc_ref[...] + skip_ref[0:1] * short_conv(0))
        z_ref[...] = z
        sig_ref[...] = z.astype(BF16)

    @pl.when((f == last) & (ph == 1))
    def _():
        o_ref[0] = (short_conv(2 * w) * (acc_ref[...] + skip_ref[1:2] * z_ref[...])).astype(BF16)


def hyena_mixer(p, short_w, short_b, skip, tabs, c, s_fwd, s_inv):
    bsz, n, _ = p.shape
    w = skip.shape[1]
    tf = min(n, 256)
    return pl.pallas_call(
        functools.partial(_hyena_kernel, w=w),
        grid=(bsz, HY_ORDER, n // tf),
        in_specs=[pl.BlockSpec((1, n, 3 * w), lambda b, o, f: (b, 0, 0)),
                  pl.BlockSpec((3, 3 * w), lambda b, o, f: (0, 0)),
                  pl.BlockSpec((1, 3 * w), lambda b, o, f: (0, 0)),
                  pl.BlockSpec((HY_ORDER, w), lambda b, o, f: (0, 0)),
                  pl.BlockSpec((tf, n), lambda b, o, f: (f, 0)),
                  pl.BlockSpec((tf, n), lambda b, o, f: (f, 0)),
                  pl.BlockSpec((n, tf), lambda b, o, f: (0, f)),
                  pl.BlockSpec((n, tf), lambda b, o, f: (0, f)),
                  pl.BlockSpec((1, 3, tf, w), lambda b, o, f: (o, 0, f, 0))],
        out_specs=pl.BlockSpec((1, n, w), lambda b, o, f: (b, 0, 0)),
        out_shape=jax.ShapeDtypeStruct((bsz, n, w), BF16),
        scratch_shapes=[pltpu.VMEM((n, w), BF16), pltpu.VMEM((n, w), F32), pltpu.VMEM((n, w), F32)],
        compiler_params=_cparams(("arbitrary", "arbitrary", "arbitrary")),
        name="hyena_mixer",
    )(p, short_w, short_b[None], skip, c, s_fwd, c, s_inv, tabs)


def rope_tables(n):
    pos = jnp.arange(n)
    rc = jnp.stack([(pos // GRID_W).astype(F32), (pos % GRID_W).astype(F32)], axis=1)
    inv = ROPE_BASE ** (-jnp.arange(ROPE_NF, dtype=F32) / ROPE_NF)
    ang = rc[:, :, None] * inv
    cos = jnp.cos(ang)
    sin = jnp.sin(ang)
    zero = jnp.zeros_like(sin)
    tile = lambda a, b: jnp.tile(jnp.stack([a, b], axis=2).reshape(n, 4 * ROPE_NF), (1, 2))
    return tile(cos, cos), tile(-sin, zero), tile(zero, sin)


def _rope(x, cos, sin_a, sin_b):
    return (x * cos + pltpu.roll(x, LANES - ROPE_NF, 1) * sin_a + pltpu.roll(x, ROPE_NF, 1) * sin_b)


def _attn_kernel(*refs, n_pre, rope, lam_init, tq):
    if n_pre:
        q_ref, k_ref, v_ref, kc_ref, vc_ref, cos_ref, sa_ref, sb_ref, lam_ref, g_ref, o_ref, kbuf = refs
    else:
        q_ref, k_ref, v_ref, cos_ref, sa_ref, sb_ref, lam_ref, g_ref, o_ref, kbuf = refs
    n = q_ref.shape[1]
    lp = lam_ref[...]
    lam = (jnp.exp(jnp.sum(lp[0:1] * lp[1:2], axis=1, keepdims=True))
           - jnp.exp(jnp.sum(lp[2:3] * lp[3:4], axis=1, keepdims=True)) + lam_init)
    k = k_ref[0].astype(F32)
    if rope:
        k = _rope(k, cos_ref[...], sa_ref[...], sb_ref[...])
    kbuf[...] = k.astype(BF16)
    gain = g_ref[...] * (1.0 - lam_init)

    def body(i, carry):
        r0 = pl.multiple_of(i * tq, tq)
        q = q_ref[0, pl.ds(r0, tq), :].astype(F32)
        if rope:
            q = _rope(q, cos_ref[pl.ds(r0, tq), :], sa_ref[pl.ds(r0, tq), :], sb_ref[pl.ds(r0, tq), :])
        q = q * (DA_HD ** -0.5)
        lane = lax.broadcasted_iota(jnp.int32, q.shape, 1)
        es, invs = [], []
        for sub in range(2):
            qm = jnp.where((lane >= DA_HD) == bool(sub), q, 0.0).astype(BF16)
            sl = _dot_nt(qm, kbuf[...])
            m = jnp.max(sl, axis=-1, keepdims=True)
            if n_pre:
                sc = _dot_nt(qm, kc_ref[0])
                m = jnp.maximum(m, jnp.max(sc, axis=-1, keepdims=True))
                ec = jnp.exp(sc - m)
            el = jnp.exp(sl - m)
            tot = jnp.sum(el, axis=-1, keepdims=True)
            if n_pre:
                tot = tot + jnp.sum(ec, axis=-1, keepdims=True)
            es.append((ec if n_pre else None, el))
            invs.append(1.0 / tot)
        c0 = invs[0]
        c1 = lam * invs[1]
        wl = (es[0][1] * c0 - es[1][1] * c1).astype(BF16)
        o = _dot(wl, v_ref[0])
        if n_pre:
            wc = (es[0][0] * c0 - es[1][0] * c1).astype(BF16)
            o = o + _dot(wc, vc_ref[0])
        o = o * lax.rsqrt(jnp.mean(o * o, axis=-1, keepdims=True) + LN_EPS) * gain
        o_ref[0, pl.ds(r0, tq), :] = o.astype(BF16)
        return carry

    lax.fori_loop(0, n // tq, body, 0)


def diff_attention(p, p_pre, tables, lam_params, subln_g, lam_init, rope):
    bsz, n, width = p.shape
    qc = (width // 2) // LANES
    nh = DA_HEADS
    n_pre = 0 if p_pre is None else p_pre.shape[1]
    tq = min(n, 256)
    col = lambda off: pl.BlockSpec((1, n, LANES), lambda b, h, off=off: (b, 0, qc + off + h))
    in_specs = [col(0), col(nh), col(2 * nh)]
    args = [p, p, p]
    if n_pre:
        pre = lambda off: pl.BlockSpec((1, n_pre, LANES), lambda b, h, off=off: (b, 0, qc + off + h))
        in_specs += [pre(nh), pre(2 * nh)]
        args += [p_pre, p_pre]
    full = lambda shape: pl.BlockSpec(shape, lambda b, h: (0,) * len(shape))
    in_specs += [full((n, LANES))] * 3 + [full((4, LANES)), full((1, LANES))]
    args += list(tables) + [lam_params, subln_g[None]]
    return pl.pallas_call(
        functools.partial(_attn_kernel, n_pre=n_pre, rope=rope, lam_init=lam_init, tq=tq),
        grid=(bsz, nh),
        in_specs=in_specs,
        out_specs=pl.BlockSpec((1, n, LANES), lambda b, h: (b, 0, h)),
        out_shape=jax.ShapeDtypeStruct((bsz, n, nh * LANES), BF16),
        scratch_shapes=[pltpu.VMEM((n, LANES), BF16)],
        compiler_params=_cparams(("arbitrary", "arbitrary")),
        name="diff_attention",
    )(*args)


def _ab_out_kernel(a_ref, b_ref, w_ref, x_ref, g_ref, lg_ref, lb_ref, o_ref):
    ka = a_ref.shape[2]
    m = _dot(a_ref[0], w_ref[0:ka, :]) + _dot(b_ref[0], w_ref[ka:, :])
    o_ref[0] = _layer_norm(DN_ALPHA * x_ref[0] + g_ref[0] * m, lg_ref[...], lb_ref[...])


def out_proj_norm(a, b, w, x, g, ln_g, ln_b):
    bsz, n, d = x.shape
    tm = min(n, 512)
    row = lambda arr: pl.BlockSpec((1, tm, arr.shape[2]), lambda bb, i: (bb, i, 0))
    vec = pl.BlockSpec((1, d), lambda bb, i: (0, 0))
    return pl.pallas_call(
        _ab_out_kernel,
        grid=(bsz, n // tm),
        in_specs=[row(a), row(b), pl.BlockSpec(w.shape, lambda bb, i: (0, 0)), row(x),
                  pl.BlockSpec((1, 1, d), lambda bb, i: (bb, 0, 0)), vec, vec],
        out_specs=row(x),
        out_shape=jax.ShapeDtypeStruct(x.shape, F32),
        compiler_params=_cparams(("arbitrary", "arbitrary")),
        name="ab_out",
    )(a, b, w, x, g, ln_g[None], ln_b[None])


def _rg_out_kernel(gate_ref, hf_ref, hb_ref, w_ref, x_ref, g_ref, lg_ref, lb_ref, o_ref):
    h = hf_ref[0].astype(F32) + hb_ref[0].astype(F32)
    m = _dot((gate_ref[0].astype(F32) * h).astype(BF16), w_ref[...])
    o_ref[0] = _layer_norm(DN_ALPHA * x_ref[0] + g_ref[0] * m, lg_ref[...], lb_ref[...])


def rg_out(gate, hf, hb, row0, w, x, g, ln_g, ln_b):
    bsz, n, d = x.shape
    tm = RG_TILE
    off = row0 // tm
    seq = pl.BlockSpec((1, tm, d), lambda bb, i: (bb, i + off, 0))
    row = pl.BlockSpec((1, tm, d), lambda bb, i: (bb, i, 0))
    vec = pl.BlockSpec((1, d), lambda bb, i: (0, 0))
    return pl.pallas_call(
        _rg_out_kernel,
        grid=(bsz, n // tm),
        in_specs=[seq, seq, seq, pl.BlockSpec(w.shape, lambda bb, i: (0, 0)), row,
                  pl.BlockSpec((1, 1, d), lambda bb, i: (bb, 0, 0)), vec, vec],
        out_specs=row,
        out_shape=jax.ShapeDtypeStruct(x.shape, F32),
        compiler_params=_cparams(("arbitrary", "arbitrary")),
        name="rg_out",
    )(gate, hf, hb, w, x, g, ln_g[None], ln_b[None])


RG_TILE = 256
RG_STEPS = 32
RG_HALO = 8


def _rg_in_kernel(xc_ref, xl_ref, scc_ref, shc_ref, scl_ref, shl_ref, w_ref, gate_ref, rec_ref, *, nct):
    d = xc_ref.shape[2]
    is_ctx = pl.program_id(1) < nct
    x = jnp.where(is_ctx, xc_ref[0], xl_ref[0])
    sc = jnp.where(is_ctx, scc_ref[0], scl_ref[0])
    sh = jnp.where(is_ctx, shc_ref[0], shl_ref[0])
    h = (x * (1.0 + sc) + sh).astype(BF16)
    y = _dot(h, w_ref[:, 0:d])
    gelu = 0.5 * y * (1.0 + jnp.tanh(math.sqrt(2.0 / math.pi) * (y + 0.044715 * (y * y * y))))
    gate_ref[0] = gelu.astype(BF16)
    rec_ref[0] = _dot(h, w_ref[:, d:])


def rg_in(xc, xl, scc, shc, scl, shl, w):
    bsz, nc, d = xc.shape
    tm = RG_TILE
    nct = nc // tm
    nt = nct + xl.shape[1] // tm
    row = pl.BlockSpec((1, tm, d), lambda b, i: (b, i, 0))
    vec = pl.BlockSpec((1, 1, d), lambda b, i: (b, 0, 0))
    return pl.pallas_call(
        functools.partial(_rg_in_kernel, nct=nct),
        grid=(bsz, nt),
        in_specs=[pl.BlockSpec((1, tm, d), lambda b, i: (b, jnp.minimum(i, nct - 1), 0)),
                  pl.BlockSpec((1, tm, d), lambda b, i: (b, jnp.maximum(i - nct, 0), 0)),
                  vec, vec, vec, vec, pl.BlockSpec(w.shape, lambda b, i: (0, 0))],
        out_specs=[row, row],
        out_shape=[jax.ShapeDtypeStruct((bsz, nt * tm, d), BF16), jax.ShapeDtypeStruct((bsz, nt * tm, d), F32)],
        compiler_params=_cparams(("arbitrary", "arbitrary")),
        name="rg_in",
    )(xc, xl, scc, shc, scl, shl, w)


def _rg_scan_kernel(cf_ref, pf_ref, nf_ref, cb_ref, pb_ref, nb_ref, cw_ref, cbias_ref, wcat_ref, bcat_ref,
                    lam_ref, hf_ref, hb_ref, u_scr, af_scr, bf_scr, ab_scr, bb_scr, hf_scr, hb_scr, carry,
                    *, nct):
    bsz, tl, d = cf_ref.shape
    bw = d // RG_BLOCKS
    i = pl.program_id(0)
    n = pl.num_programs(0)

    @pl.when(i == 0)
    def _():
        carry[...] = jnp.zeros_like(carry)

    tiles = (i, jnp.where(i < nct, nct - 1 - i, n - 1 + nct - i))
    srcs = ((cf_ref, pf_ref, nf_ref, af_scr, bf_scr), (cb_ref, pb_ref, nb_ref, ab_scr, bb_scr))
    cw = cw_ref[...]
    for dr in range(2):
        cur, prev, nxt, a_scr, b_scr = srcs[dr]
        tile = tiles[dr]
        first = (tile == 0) | (tile == nct)
        last = (tile == nct - 1) | (tile == n - 1)

        def conv_b(b, c, cur=cur, prev=prev, nxt=nxt, first=first, last=last):
            x = cur[b]
            e = jnp.concatenate([jnp.where(first, 0.0, prev[b]), x, jnp.where(last, 0.0, nxt[b])], axis=0)
            m = tl + 2 * RG_HALO
            sh = lambda k: pltpu.roll(e, k % m, 0)[RG_HALO:RG_HALO + tl]
            u_scr[b] = cw[0:1] * sh(2) + cw[1:2] * sh(1) + cw[2:3] * x + cw[3:4] * sh(-1) + cbias_ref[...]
            return c

        lax.fori_loop(0, bsz, conv_b, 0)
        u = u_scr[...].reshape(bsz * tl, d)
        ub = u.astype(BF16)
        lam = lam_ref[dr:dr + 1, :]
        log_sig = jnp.minimum(lam, 0.0) - jnp.log(1.0 + jnp.exp(-jnp.abs(lam)))
        for kb in range(RG_BLOCKS):
            sl = slice(kb * bw, (kb + 1) * bw)
            y = _dot(ub[:, sl], wcat_ref[dr, kb]) + bcat_ref[dr, kb]
            r = _sigmoid(y[:, :bw])
            g = _sigmoid(y[:, bw:])
            a = jnp.exp(RG_C * r * log_sig[:, sl])
            a_scr[:, :, sl] = a.reshape(bsz, tl, bw)
            b_scr[:, :, sl] = (jnp.sqrt(1.0 - a * a) * (g * u[:, sl])).reshape(bsz, tl, bw)

    def step(s, hs):
        hf, hb = hs
        tb = tl - 1 - s
        hf = af_scr[:, s, :] * hf + bf_scr[:, s, :]
        hf_scr[:, s, :] = hf
        hb = ab_scr[:, tb, :] * hb + bb_scr[:, tb, :]
        hb_scr[:, tb, :] = hb
        return hf, hb

    hf, hb = lax.fori_loop(0, tl, step, (carry[0], carry[1]), unroll=2)
    carry[0] = hf
    carry[1] = hb
    hf_ref[...] = hf_scr[...].astype(BF16)
    hb_ref[...] = hb_scr[...].astype(BF16)


def rg_scan(rec, n_ctx, conv_w, conv_b, wcat, bcat, lam):
    bsz, n, d = rec.shape
    tl = RG_STEPS
    nt = n // tl
    nct = n_ctx // tl
    per = tl // RG_HALO
    fmap = lambda i: i
    bmap = lambda i: jnp.where(i < nct, nct - 1 - i, nt - 1 + nct - i)

    def views(tmap):
        return [pl.BlockSpec((bsz, tl, d), lambda i: (0, tmap(i), 0)),
                pl.BlockSpec((bsz, RG_HALO, d), lambda i: (0, jnp.maximum(tmap(i) * per - 1, 0), 0)),
                pl.BlockSpec((bsz, RG_HALO, d), lambda i: (0, jnp.minimum((tmap(i) + 1) * per, n // RG_HALO - 1), 0))]

    full = lambda arr: pl.BlockSpec(arr.shape, lambda i: (0,) * arr.ndim)
    cb = conv_b[None]
    slab = pltpu.VMEM((bsz, tl, d), F32)
    return pl.pallas_call(
        functools.partial(_rg_scan_kernel, nct=nct),
        grid=(nt,),
        in_specs=views(fmap) + views(bmap) + [full(conv_w), full(cb), full(wcat), full(bcat), full(lam)],
        out_specs=[pl.BlockSpec((bsz, tl, d), lambda i: (0, fmap(i), 0)),
                   pl.BlockSpec((bsz, tl, d), lambda i: (0, bmap(i), 0))],
        out_shape=[jax.ShapeDtypeStruct((bsz, n, d), BF16)] * 2,
        scratch_shapes=[slab] * 7 + [pltpu.VMEM((2, bsz, d), F32)],
        compiler_params=_cparams(("arbitrary",)),
        name="rg_scan",
    )(rec, rec, rec, rec, rec, rec, conv_w, cb, wcat, bcat, lam)


def _route_kernel(x_ref, sc_ref, sh_ref, wh_ref, wl_ref, bias_ref, gates_ref, rank_ref, rank_t_ref):
    t = x_ref[0] * (1.0 + sc_ref[0]) + sh_ref[0]
    th = t.astype(BF16)
    tl = (t - th.astype(F32)).astype(BF16)
    logits = _dot_nt(wh_ref[...], th) + (_dot_nt(wh_ref[...], tl) + _dot_nt(wl_ref[...], th))
    scores = jax.nn.sigmoid(logits)
    sel = scores + bias_ref[...]
    ninf = -jnp.inf
    sub = lax.broadcasted_iota(jnp.int32, (EPG, sel.shape[1]), 0)
    gscore = []
    for g in range(N_GROUPS):
        slab = sel[g * EPG:(g + 1) * EPG]
        m1 = jnp.max(slab, axis=0, keepdims=True)
        first = jnp.min(jnp.where(slab == m1, sub, EPG), axis=0, keepdims=True)
        m2 = jnp.max(jnp.where(sub == first, ninf, slab), axis=0, keepdims=True)
        gscore.append(m1 + m2)
    slabs = []
    for g in range(N_GROUPS):
        beaten = jnp.zeros(gscore[g].shape, F32)
        for o in range(N_GROUPS):
            if o != g:
                wins = (gscore[o] >= gscore[g]) if o < g else (gscore[o] > gscore[g])
                beaten = beaten + jnp.where(wins, 1.0, 0.0)
        keep = jnp.broadcast_to(beaten < TOPK_GROUPS, (EPG, sel.shape[1]))
        slabs.append(jnp.where(keep, sel[g * EPG:(g + 1) * EPG], ninf))
    v = jnp.concatenate(slabs, axis=0)
    eidx = lax.broadcasted_iota(jnp.int32, v.shape, 0)
    w = jnp.zeros(v.shape, F32)
    chosen = jnp.zeros(v.shape, F32)
    for _ in range(TOP_K):
        m = jnp.max(v, axis=0, keepdims=True)
        first = jnp.min(jnp.where(v == m, eidx, N_EXPERTS), axis=0, keepdims=True)
        hit = eidx == first
        w = jnp.where(hit, scores, w)
        chosen = jnp.where(hit, 1.0, chosen)
        v = jnp.where(hit, ninf, v)
    gates = w / jnp.sum(w, axis=0, keepdims=True) * ROUTED_SCALE
    tm = v.shape[1]
    upper = jnp.where(lax.broadcasted_iota(jnp.int32, (tm, tm), 0) < lax.broadcasted_iota(jnp.int32, (tm, tm), 1),
                      1.0, 0.0).astype(BF16)
    rank = jnp.where(chosen > 0.0, _dot(chosen.astype(BF16), upper), -1.0)
    pad = LANES - N_EXPERTS
    gates_ref[...] = jnp.concatenate([gates, jnp.zeros((pad, tm), F32)], axis=0).T
    rank_p = jnp.concatenate([rank, jnp.full((pad, tm), -1.0, F32)], axis=0)
    rank_ref[...] = rank_p.T.astype(BF16)
    rank_t_ref[...] = rank_p.astype(BF16)


MOE_WIN = 256
SEG_ALIGN = 16
MOE_TILE = 512
WIN_ROWS = -(-(MOE_WIN * TOP_K + N_EXPERTS * (SEG_ALIGN - 1)) // 256) * 256
SEG_SIZES = tuple(SEG_ALIGN << k for k in reversed(range((MOE_WIN // SEG_ALIGN).bit_length())))
WAIT_SIZES = tuple(SEG_ALIGN << k for k in reversed(range((WIN_ROWS // SEG_ALIGN).bit_length())))


def route(x, sc, sh, rw_hi, rw_lo, bias):
    bsz, n, d = x.shape
    tm = MOE_WIN
    nt = n // tm
    full = lambda arr: pl.BlockSpec(arr.shape, lambda b, i: (0,) * arr.ndim)
    vec = pl.BlockSpec((1, 1, d), lambda b, i: (b, 0, 0))
    tok = pl.BlockSpec((tm, LANES), lambda b, i: (b * nt + i, 0))
    return pl.pallas_call(
        _route_kernel,
        grid=(bsz, nt),
        in_specs=[pl.BlockSpec((1, tm, d), lambda b, i: (b, i, 0)), vec, vec, full(rw_hi), full(rw_lo),
                  full(bias)],
        out_specs=[tok, tok, pl.BlockSpec((LANES, tm), lambda b, i: (0, b * nt + i))],
        out_shape=[jax.ShapeDtypeStruct((bsz * n, LANES), F32), jax.ShapeDtypeStruct((bsz * n, LANES), BF16),
                   jax.ShapeDtypeStruct((LANES, bsz * n), BF16)],
        compiler_params=_cparams(("arbitrary", "arbitrary")),
        name="route",
    )(x, sc, sh, rw_hi, rw_lo, bias)


def moe_plan(rank_tm, nw):
    cnt = (rank_tm[:, :N_EXPERTS] >= 0).reshape(nw, MOE_WIN, N_EXPERTS).sum(axis=1).astype(jnp.int32)
    pc = (cnt + SEG_ALIGN - 1) // SEG_ALIGN * SEG_ALIGN
    lo = jnp.cumsum(pc, axis=1) - pc
    used = pc.sum(axis=0)
    tot = (used + MOE_TILE - 1) // MOE_TILE * MOE_TILE
    ends = jnp.cumsum(tot)
    go = (ends - tot)[None, :] + jnp.cumsum(pc, axis=0) - pc
    n_tiles = (nw * MOE_WIN * TOP_K + nw * N_EXPERTS * (SEG_ALIGN - 1)) // MOE_TILE + N_EXPERTS
    tile_expert = jnp.sum(jnp.arange(n_tiles, dtype=jnp.int32)[:, None] >= (ends // MOE_TILE)[None, :], axis=1)
    col = lambda a: jnp.pad(a.astype(F32), ((0, 0), (0, LANES - N_EXPERTS)))
    return dict(
        lo=lo.reshape(-1), pc=pc.reshape(-1), go=go.reshape(-1),
        tail_pc=tot - used, tail_go=ends - tot + used, used=pc.sum(axis=1),
        lo_row=col(lo)[:, None, :], pc_row=col(pc)[:, None, :],
        lo_col=col(lo)[:, :, None], pc_col=col(pc)[:, :, None],
        tile_expert=jnp.minimum(tile_expert, N_EXPERTS - 1).astype(jnp.int32),
        n_used=(ends[-1] // MOE_TILE).astype(jnp.int32).reshape(1), n_tiles=n_tiles)


def _segment_copies(lo_s, pc_s, go_s, w, local_ref, sorted_hbm, sem, to_sorted, wait):
    def body(e, carry):
        idx = w * N_EXPERTS + e
        lo = 0 if lo_s is None else lo_s[idx]
        pc = pc_s[idx]
        go = go_s[idx]
        off = jnp.int32(0)
        for size in SEG_SIZES:
            bit = pc & size

            @pl.when(bit != 0)
            def _(off=off, size=size):
                loc = local_ref.at[pl.ds(pl.multiple_of(lo + off, SEG_ALIGN), size), :]
                glo = sorted_hbm.at[pl.ds(pl.multiple_of(go + off, SEG_ALIGN), size), :]
                cp = pltpu.make_async_copy(loc, glo, sem) if to_sorted else pltpu.make_async_copy(glo, loc, sem)
                if wait:
                    cp.wait()
                else:
                    cp.start()

            off = off + bit
        return carry

    lax.fori_loop(0, N_EXPERTS, body, 0)


def _wait_rows(rows, local_ref, sorted_hbm, sem, to_sorted):
    for size in WAIT_SIZES:
        @pl.when((rows & size) != 0)
        def _(size=size):
            loc = local_ref.at[pl.ds(0, size), :]
            glo = sorted_hbm.at[pl.ds(0, size), :]
            cp = pltpu.make_async_copy(loc, glo, sem) if to_sorted else pltpu.make_async_copy(glo, loc, sem)
            cp.wait()


def _dispatch_kernel(lo_s, pc_s, go_s, tpc_s, tgo_s, used_s, x_ref, sc_ref, sh_ref, rank_t_ref, lo_ref,
                     pc_ref, sorted_hbm, rows_ref, sem):
    w = pl.program_id(0) * pl.num_programs(1) + pl.program_id(1)

    @pl.when(w == 0)
    def _():
        rows_ref[0:MOE_TILE, :] = jnp.zeros((MOE_TILE, rows_ref.shape[1]), BF16)
        _segment_copies(None, tpc_s, tgo_s, 0, rows_ref, sorted_hbm, sem.at[0], True, False)
        _segment_copies(None, tpc_s, tgo_s, 0, rows_ref, sorted_hbm, sem.at[0], True, True)

    t = (x_ref[0] * (1.0 + sc_ref[0]) + sh_ref[0]).astype(BF16)
    lo = lo_ref[0]
    hi = lo + pc_ref[0]
    chunk = 512
    for c in range(WIN_ROWS // chunk):
        r = (lax.broadcasted_iota(jnp.int32, (chunk, LANES), 0) + c * chunk).astype(F32)
        owner = (r >= lo) & (r < hi)
        seg_lo = jnp.sum(jnp.where(owner, lo, 0.0), axis=1, keepdims=True)
        rank_of = _dot(jnp.where(owner, 1.0, 0.0).astype(BF16), rank_t_ref[...])
        pick = jnp.where(rank_of == r[:, 0:1] - seg_lo, 1.0, 0.0).astype(BF16)
        rows_ref[c * chunk:(c + 1) * chunk, :] = _dot(pick, t).astype(BF16)
    _segment_copies(lo_s, pc_s, go_s, w, rows_ref, sorted_hbm, sem.at[0], True, False)
    _wait_rows(used_s[w], rows_ref, sorted_hbm, sem.at[0], True)


def moe_dispatch(x, sc, sh, rank_t, plan, n_rows):
    bsz, n, d = x.shape
    nwb = n // MOE_WIN
    vec = pl.BlockSpec((1, 1, d), lambda b, i, *_: (b, 0, 0))
    meta = pl.BlockSpec((1, 1, LANES), lambda b, i, *_: (b * nwb + i, 0, 0))
    return pl.pallas_call(
        _dispatch_kernel,
        grid_spec=pltpu.PrefetchScalarGridSpec(
            num_scalar_prefetch=6,
            grid=(bsz, nwb),
            in_specs=[pl.BlockSpec((1, MOE_WIN, d), lambda b, i, *_: (b, i, 0)), vec, vec,
                      pl.BlockSpec((LANES, MOE_WIN), lambda b, i, *_: (0, b * nwb + i)), meta, meta],
            out_specs=pl.BlockSpec(memory_space=pl.ANY),
            scratch_shapes=[pltpu.VMEM((WIN_ROWS, d), BF16), pltpu.SemaphoreType.DMA((1,))]),
        out_shape=jax.ShapeDtypeStruct((n_rows, d), BF16),
        compiler_params=_cparams(("arbitrary", "arbitrary")),
        name="moe_dispatch",
    )(plan["lo"], plan["pc"], plan["go"], plan["tail_pc"], plan["tail_go"], plan["used"], x, sc, sh, rank_t,
      plan["lo_row"], plan["pc_row"])


def _expert_kernel(te_s, nu_s, x_ref, wg_ref, wu_ref, wd_ref, o_ref):
    @pl.when(pl.program_id(0) < nu_s[0])
    def _():
        x = x_ref[...]
        h = _silu(_dot(x, wg_ref[0, 0].astype(BF16))) * _dot(x, wu_ref[0, 0].astype(BF16))
        o_ref[...] = _dot(h.astype(BF16), wd_ref[0, 0].astype(BF16)).astype(BF16)


def moe_experts(xs, plan, layer, wg, wu, wd):
    n_rows, d = xs.shape
    de = wg.shape[3]
    tile = lambda i, te, nu: jnp.minimum(i, nu[0] - 1)
    row = pl.BlockSpec((MOE_TILE, d), lambda i, te, nu: (tile(i, te, nu), 0))
    return pl.pallas_call(
        _expert_kernel,
        grid_spec=pltpu.PrefetchScalarGridSpec(
            num_scalar_prefetch=2,
            grid=(plan["n_tiles"],),
            in_specs=[row,
                      pl.BlockSpec((1, 1, d, de), lambda i, te, nu: (layer, te[tile(i, te, nu)], 0, 0)),
                      pl.BlockSpec((1, 1, d, de), lambda i, te, nu: (layer, te[tile(i, te, nu)], 0, 0)),
                      pl.BlockSpec((1, 1, de, d), lambda i, te, nu: (layer, te[tile(i, te, nu)], 0, 0))],
            out_specs=row),
        out_shape=jax.ShapeDtypeStruct((n_rows, d), BF16),
        compiler_params=_cparams(("arbitrary",)),
        name="moe_experts",
    )(plan["tile_expert"], plan["n_used"], xs, wg, wu, wd)


def _combine_kernel(lo_s, pc_s, go_s, used_s, x_ref, sc_ref, sh_ref, gf_ref, gates_ref, rank_ref, lo_ref, pc_ref,
                    sorted_hbm, sg_ref, su_ref, sd_ref, lg_ref, lb_ref, o_ref, rows_ref, sem):
    first = (pl.program_id(0) == 0) & (pl.program_id(1) == 0)

    @pl.when(first)
    def _():
        rows_ref[...] = jnp.zeros_like(rows_ref)

    w = pl.program_id(0) * pl.num_programs(1) + pl.program_id(1)
    _segment_copies(lo_s, pc_s, go_s, w, rows_ref, sorted_hbm, sem.at[0], False, False)
    x = x_ref[0]
    t = (x * (1.0 + sc_ref[0]) + sh_ref[0]).astype(BF16)
    shared = _dot((_silu(_dot(t, sg_ref[0].astype(BF16))) * _dot(t, su_ref[0].astype(BF16))).astype(BF16),
                  sd_ref[0].astype(BF16))
    r = lax.broadcasted_iota(jnp.int32, (LANES, WIN_ROWS), 1).astype(F32)
    lo = lo_ref[0]
    owner = (r >= lo) & (r < lo + pc_ref[0])
    seg_lo = jnp.sum(jnp.where(owner, lo, 0.0), axis=0, keepdims=True)
    owner_b = jnp.where(owner, 1.0, 0.0).astype(BF16)
    rank_of = _dot(rank_ref[...], owner_b)
    gate_of = _dot(gates_ref[...].astype(BF16), owner_b)
    weights = jnp.where(rank_of == r[0:1, :] - seg_lo, gate_of, 0.0).astype(BF16)
    _wait_rows(used_s[w], rows_ref, sorted_hbm, sem.at[0], False)
    routed = _dot(weights, rows_ref[...])
    o_ref[0] = _layer_norm(DN_ALPHA * x + gf_ref[0] * (shared + routed), lg_ref[...], lb_ref[...])


def moe_combine(x, sc, sh, gf, gates, rank_tm, plan, ys, layer, sg, su, sd, ln_g, ln_b):
    bsz, n, d = x.shape
    nwb = n // MOE_WIN
    row = pl.BlockSpec((1, MOE_WIN, d), lambda b, i, *_: (b, i, 0))
    vec = pl.BlockSpec((1, 1, d), lambda b, i, *_: (b, 0, 0))
    tok = pl.BlockSpec((MOE_WIN, LANES), lambda b, i, *_: (b * nwb + i, 0))
    meta = pl.BlockSpec((1, LANES, 1), lambda b, i, *_: (b * nwb + i, 0, 0))
    lnv = pl.BlockSpec((1, d), lambda b, i, *_: (0, 0))
    return pl.pallas_call(
        _combine_kernel,
        grid_spec=pltpu.PrefetchScalarGridSpec(
            num_scalar_prefetch=4,
            grid=(bsz, nwb),
            in_specs=[row, vec, vec, vec, tok, tok, meta, meta,
                      pl.BlockSpec(memory_space=pl.ANY),
                      pl.BlockSpec((1, d, sg.shape[2]), lambda b, i, *_: (layer, 0, 0)),
                      pl.BlockSpec((1, d, su.shape[2]), lambda b, i, *_: (layer, 0, 0)),
                      pl.BlockSpec((1, sd.shape[1], d), lambda b, i, *_: (layer, 0, 0)),
                      lnv, lnv],
            out_specs=row,
            scratch_shapes=[pltpu.VMEM((WIN_ROWS, d), BF16), pltpu.SemaphoreType.DMA((1,))]),
        out_shape=jax.ShapeDtypeStruct(x.shape, F32),
        compiler_params=_cparams(("arbitrary", "arbitrary")),
        name="moe_combine",
    )(plan["lo"], plan["pc"], plan["go"], plan["used"], x, sc, sh, gf, gates, rank_tm, plan["lo_col"],
      plan["pc_col"], ys,
      sg, su, sd, ln_g[None], ln_b[None])


def kernel(x, c, ctx, c_ctx, w_mod, b_mod, ln1_g, ln1_b, ln2_g, ln2_b, ab_w_in, ab_w_out, hy_short_w, hy_short_b, hy_fw1, hy_fb1, hy_fw2, hy_fb2, hy_fw3, hy_fb3, hy_fw4, hy_freq, hy_decay, hy_skip, da_lq1, da_lk1, da_lq2, da_lk2, da_subln_g, rg_w_in, rg_w_out, rg_conv_w, rg_conv_b, rg_wa, rg_ba, rg_wx, rg_bx, rg_lambda, router_w, router_bias, ex_w_gate, ex_w_up, ex_w_down, sh_w_gate, sh_w_up, sh_w_down):
    return forward(dict(locals()))


def split_mod(mod_l, bsz, d):
    ml = [mod_l[:bsz, k * d:(k + 1) * d][:, None, :] for k in range(N_MOD)]
    mc = [jnp.broadcast_to(mod_l[bsz, k * d:(k + 1) * d][None, None, :], (bsz, 1, d)) for k in range(N_MOD)]
    return ml, mc


def mixer_even(p, l, xs, cs, ml, mc, need_ctx):
    e = l // 2
    n_lat, n_ctx = xs.shape[1], cs.shape[1]
    lam_init = 0.8 - 0.6 * math.exp(-0.3 * l)
    w_in = p["ab_w_in"][e].astype(BF16)
    w_out = p["ab_w_out"][e].astype(BF16)
    p_lat = ab_in(xs, ml[1], ml[0], w_in)
    p_ctx = ab_in(cs, mc[1], mc[0], w_in)
    filt_p = tuple(p[k][e] for k in ("hy_fw1", "hy_fb1", "hy_fw2", "hy_fb2", "hy_fw3", "hy_fb3", "hy_fw4",
                                     "hy_freq", "hy_decay"))
    lam_params = jnp.pad(jnp.stack([p["da_lq1"][e], p["da_lk1"][e], p["da_lq2"][e], p["da_lk2"][e]]),
                         ((0, 0), (0, LANES - DA_HD)))
    rope_lat = rope_tables(n_lat)
    rope_ctx = tuple(t[:n_ctx] for t in rope_lat)

    def hyena(proj, n):
        dft = dft_tables(n)
        hs, hd = hyena_filter_parts(n, *filt_p)
        tabs = filter_spectrum(hs, hd, dft[0], dft[1])
        return hyena_mixer(proj, p["hy_short_w"][e], p["hy_short_b"][e], p["hy_skip"][e], tabs, *dft)

    g1, b1 = p["ln1_g"][l], p["ln1_b"][l]
    sub_g = p["da_subln_g"][e]
    y_lat = hyena(p_lat, n_lat)
    o_lat = diff_attention(p_lat, p_ctx, rope_lat, lam_params, sub_g, lam_init, True)
    xs_new = out_proj_norm(y_lat, o_lat, w_out, xs, ml[2], g1, b1)
    if need_ctx:
        y_ctx = hyena(p_ctx, n_ctx)
        o_ctx = diff_attention(p_ctx, None, rope_ctx, lam_params, sub_g, lam_init, False)
        cs = out_proj_norm(y_ctx, o_ctx, w_out, cs, mc[2], g1, b1)
    return xs_new, cs


def mixer_odd(p, l, xs, cs, ml, mc, need_ctx):
    o = l // 2
    w_in = p["rg_w_in"][o].astype(BF16)
    w_out = p["rg_w_out"][o].astype(BF16)
    n_ctx = cs.shape[1]
    gate, rec = rg_in(cs, xs, mc[1], mc[0], ml[1], ml[0], w_in)
    wcat = jnp.concatenate([p["rg_wa"][o], p["rg_wx"][o]], axis=-1).astype(BF16)
    split = lambda v: v.reshape(2, RG_BLOCKS, 1, -1)
    bcat = jnp.concatenate([split(p["rg_ba"][o]), split(p["rg_bx"][o])], axis=-1)
    hf, hb = rg_scan(rec, n_ctx, p["rg_conv_w"][o], p["rg_conv_b"][o], wcat, bcat, p["rg_lambda"][o])
    g1, b1 = p["ln1_g"][l], p["ln1_b"][l]
    xs = rg_out(gate, hf, hb, n_ctx, w_out, xs, ml[2], g1, b1)
    if need_ctx:
        cs = rg_out(gate, hf, hb, 0, w_out, cs, mc[2], g1, b1)
    return xs, cs


def moe_block(p, l, h, m):
    rw_t = p["router_w"][l].T
    rw_hi = rw_t.astype(BF16)
    rw_lo = (rw_t - rw_hi.astype(F32)).astype(BF16)
    gates, rank_tm, rank_t = route(h, m[4], m[3], rw_hi, rw_lo, p["router_bias"][l][:, None])
    plan = moe_plan(rank_tm, gates.shape[0] // MOE_WIN)
    xs = moe_dispatch(h, m[4], m[3], rank_t, plan, plan["n_tiles"] * MOE_TILE)
    ys = moe_experts(xs, plan, l, p["ex_w_gate"], p["ex_w_up"], p["ex_w_down"])
    return moe_combine(h, m[4], m[3], m[5], gates, rank_tm, plan, ys, l, p["sh_w_gate"], p["sh_w_up"],
                       p["sh_w_down"], p["ln2_g"][l], p["ln2_b"][l])


def run_layer(p, l, xs, cs, mod_l):
    bsz, _, d = xs.shape
    need_ctx = l < DEPTH - 1
    ml, mc = split_mod(mod_l, bsz, d)
    mixer = mixer_even if l % 2 == 0 else mixer_odd
    xs, cs = mixer(p, l, xs, cs, ml, mc, need_ctx)
    xs = moe_block(p, l, xs, ml)
    if need_ctx:
        cs = moe_block(p, l, cs, mc)
    return xs, cs


def conditioning_rows(c, c_ctx):
    bsz, d = c.shape
    rows = 8 * ((bsz + 1 + 7) // 8)
    return jnp.zeros((rows, d), F32).at[:bsz].set(c).at[bsz].set(c_ctx)


def forward(p):
    mod = modulation(conditioning_rows(p["c"], p["c_ctx"]), p["w_mod"], p["b_mod"])
    xs, cs = p["x"], p["ctx"]
    for l in range(DEPTH):
        xs, cs = run_layer(p, l, xs, cs, mod[l])
    return xs
```
